```python
import math
import jax, jax.numpy as jnp
from jax import lax
import numpy as np

D_MODEL = 1024
BATCH = 8
SEQ = 2048
DEPTH = 4
DEC_BATCH = 128
DEC_SEQ = 1
PAST_LEN = 2048
PAGE_SIZE = 128

N_MIXERS = 2
N_CONV_LAYERS = (DEPTH + 1) // 2
N_ATTN_LAYERS = DEPTH // 2
HEAD_DIM = 64
HEADS_PER_GROUP = D_MODEL // HEAD_DIM
WINDOWS = (128, 512, 2048)
DILATIONS = (1, 4, 16)
N_GROUPS = len(WINDOWS)
BLK = 128
SCALE = HEAD_DIM ** -0.5
NUM_BUCKETS = 32
MAX_DISTANCE = 2048
D_CONV = D_MODEL
CONV_WIDTH = 31
D_FF = 4 * D_MODEL
EPS = 1e-6

kernel_name = "hybrid_conformerconv_dilatedattn_decoder_step"


def rmsnorm(x, g):
    x32 = x.astype(jnp.float32)
    y = x32 * lax.rsqrt(jnp.mean(x32 * x32, axis=-1, keepdims=True) + EPS) * g.astype(jnp.float32)
    return y.astype(x.dtype)


def t5_bucket(dist):
    max_exact = NUM_BUCKETS // 2
    n = jnp.maximum(dist.astype(jnp.float32), 1.0)
    large = max_exact + (jnp.log(n / max_exact) / math.log(MAX_DISTANCE / max_exact)
                         * (NUM_BUCKETS - max_exact)).astype(jnp.int32)
    large = jnp.minimum(large, NUM_BUCKETS - 1)
    return jnp.where(dist < max_exact, dist, large)


def group_bias(rel_bias, g):
    d = DILATIONS[g]
    w_sub = WINDOWS[g] // d
    dist = jnp.arange(w_sub + 1, dtype=jnp.int32) * d
    tbl = rel_bias[:, g * HEADS_PER_GROUP:(g + 1) * HEADS_PER_GROUP].astype(jnp.float32)
    return tbl[t5_bucket(dist)].T


def dilated_group_prompt(q, k, v, g, bias):
    B, S, H, Dh = q.shape
    d = DILATIONS[g]
    w_sub = WINDOWS[g] // d
    L = S // d
    nb = -(-L // BLK)
    Lp = nb * BLK

    def sub(a):
        a = a.reshape(B, L, d, H, Dh)
        a = jnp.pad(a, ((0, 0), (0, Lp - L), (0, 0), (0, 0), (0, 0)))
        return a.reshape(B, nb, BLK, d, H, Dh)

    def with_prev(a):
        prev = jnp.pad(a[:, :-1], ((0, 0), (1, 0), (0, 0), (0, 0), (0, 0), (0, 0)))
        return jnp.concatenate([prev, a], axis=2)

    qb = sub(q)
    kk = with_prev(sub(k))
    vv = with_prev(sub(v))
    s = jnp.einsum('bnqrhd,bnkrhd->bnrhqk', qb, kk).astype(jnp.float32) * SCALE
    qi = jnp.arange(BLK)[:, None]
    kj = jnp.arange(2 * BLK)[None, :]
    delta = qi + BLK - kj
    key_idx = jnp.arange(nb)[:, None, None] * BLK + kj[None] - BLK
    valid = (delta >= 0) & (delta <= w_sub) & (key_idx >= 0)
    s = s + bias[:, jnp.clip(delta, 0, w_sub)]
    s = jnp.where(valid[None, :, None, None], s, -jnp.inf)
    m = jnp.max(s, axis=-1, keepdims=True)
    p = jnp.exp(s - m)
    den = jnp.sum(p, axis=-1)
    o = jnp.einsum('bnrhqk,bnkrhd->bnrhqd', p, vv.astype(jnp.float32)) / den[..., None]
    lse = m[..., 0] + jnp.log(den)
    o = jnp.transpose(o, (0, 1, 4, 2, 3, 5)).reshape(B, Lp, d, H, Dh)[:, :L].reshape(B, S, H, Dh)
    lse = jnp.transpose(lse, (0, 1, 4, 2, 3)).reshape(B, Lp, d, H)[:, :L].reshape(B, S, H)
    return o, lse


def dilated_group_sample(q, kv_new, kv_cache, g, bias):
    DB, T, H, Dh = q.shape
    d = DILATIONS[g]
    w_sub = WINDOWS[g] // d
    W = kv_cache.shape[1]
    idx = W + jnp.arange(T)[:, None] - jnp.arange(w_sub + 1)[None, :] * d
    valid = idx >= 0
    from_cache = kv_cache[:, jnp.clip(idx, 0, W - 1)]
    from_new = kv_new[:, jnp.clip(idx - W, 0, T - 1)]
    kvg = jnp.where((idx < W)[None, :, :, None, None, None], from_cache, from_new)
    kg, vg = kvg[:, :, :, 0], kvg[:, :, :, 1]
    s = jnp.einsum('bthd,btjhd->bthj', q, kg).astype(jnp.float32) * SCALE + bias[None, None]
    s = jnp.where(valid[None, :, None, :], s, -jnp.inf)
    m = jnp.max(s, axis=-1, keepdims=True)
    p = jnp.exp(s - m)
    den = jnp.sum(p, axis=-1)
    o = jnp.einsum('bthj,btjhd->bthd', p, vg.astype(jnp.float32)) / den[..., None]
    lse = m[..., 0] + jnp.log(den)
    return o, lse


def dilated_attention(h, w_qkv, w_o, biases, caches):
    B, S, _ = h.shape
    qkv = (h @ w_qkv).reshape(B, S, N_GROUPS, 3, HEADS_PER_GROUP, HEAD_DIM)
    outs, lses, kv_out = [], [], []
    for g in range(N_GROUPS):
        q, k, v = qkv[:, :, g, 0], qkv[:, :, g, 1], qkv[:, :, g, 2]
        kv_rows = qkv[:, :, g, 1:3].transpose(0, 1, 2, 3, 4)
        if caches is None:
            o, lse = dilated_group_prompt(q, k, v, g, biases[g])
            keep = min(WINDOWS[g], S)
            kv_out.append(kv_rows[:, S - keep:])
        else:
            o, lse = dilated_group_sample(q, kv_rows, caches[g], g, biases[g])
            kv_out.append(kv_rows)
        outs.append(o)
        lses.append(lse)
    alpha = jax.nn.softmax(jnp.stack(lses), axis=0)
    o = jnp.sum(alpha[..., None] * jnp.stack(outs), axis=0)
    y = o.reshape(B, S, HEADS_PER_GROUP * HEAD_DIM).astype(h.dtype) @ w_o
    return y, kv_out


def conformer_conv(h, hist, w_pw1, b_pw1, w_dw, b_dw, ln_g, ln_b, w_pw2, b_pw2):
    u = h @ w_pw1 + b_pw1
    a = u[..., :D_CONV] * jax.nn.sigmoid(u[..., D_CONV:])
    full = jnp.concatenate([hist.astype(a.dtype), a], axis=1)
    c = lax.conv_general_dilated(full, w_dw[:, None, :].astype(full.dtype), (1,), 'VALID',
                                 dimension_numbers=('NWC', 'WIO', 'NWC'),
                                 feature_group_count=D_CONV) + b_dw
    c32 = c.astype(jnp.float32)
    mu = jnp.mean(c32, axis=-1, keepdims=True)
    var = jnp.mean(jnp.square(c32 - mu), axis=-1, keepdims=True)
    z = (c32 - mu) * lax.rsqrt(var + EPS) * ln_g.astype(jnp.float32) + ln_b.astype(jnp.float32)
    z = z * jax.nn.sigmoid(z)
    y = z.astype(h.dtype) @ w_pw2 + b_pw2
    return y, full[:, -(CONV_WIDTH - 1):]


def sqrelu_mlp(h, w_in, w_out):
    u = jax.nn.relu(h @ w_in)
    return (u * u) @ w_out


def setup_inputs(seed: int = 0) -> dict:
    key = jax.random.key(seed)
    ks = jax.random.split(key, 24)
    f32 = jnp.float32
    Hd = HEADS_PER_GROUP * HEAD_DIM

    def nrm(k, shape, scale):
        return jax.random.normal(k, shape, f32) * scale

    def win_cache(k, w):
        return nrm(k, (N_ATTN_LAYERS, DEC_BATCH, min(w, PAST_LEN), 2, HEADS_PER_GROUP, HEAD_DIM), 1.0)

    return {
        "x_prompt": nrm(ks[0], (BATCH, SEQ, D_MODEL), 1.0),
        "x_sample": nrm(ks[1], (DEC_BATCH, DEC_SEQ, D_MODEL), 1.0),
        "state_conv": nrm(ks[2], (N_CONV_LAYERS, DEC_BATCH, CONV_WIDTH - 1, D_CONV), 0.5),
        "cache_kv_w128": win_cache(ks[3], WINDOWS[0]),
        "cache_kv_w512": win_cache(ks[4], WINDOWS[1]),
        "cache_kv_w2048": win_cache(ks[5], WINDOWS[2]),
        "norm_mix": 1.0 + nrm(ks[6], (DEPTH, D_MODEL), 0.02),
        "norm_mlp": 1.0 + nrm(ks[7], (DEPTH, D_MODEL), 0.02),
        "norm_final": 1.0 + nrm(ks[8], (D_MODEL,), 0.02),
        "conv_w_pw1": nrm(ks[9], (N_CONV_LAYERS, D_MODEL, 2 * D_CONV), D_MODEL ** -0.5),
        "conv_b_pw1": nrm(ks[10], (N_CONV_LAYERS, 2 * D_CONV), 0.02),
        "conv_w_dw": nrm(ks[11], (N_CONV_LAYERS, CONV_WIDTH, D_CONV), CONV_WIDTH ** -0.5),
        "conv_b_dw": nrm(ks[12], (N_CONV_LAYERS, D_CONV), 0.02),
        "conv_ln_g": 1.0 + nrm(ks[13], (N_CONV_LAYERS, D_CONV), 0.02),
        "conv_ln_b": nrm(ks[14], (N_CONV_LAYERS, D_CONV), 0.02),
        "conv_w_pw2": nrm(ks[15], (N_CONV_LAYERS, D_CONV, D_MODEL), D_CONV ** -0.5),
        "conv_b_pw2": nrm(ks[16], (N_CONV_LAYERS, D_MODEL), 0.02),
        "attn_w_qkv": nrm(ks[17], (N_ATTN_LAYERS, D_MODEL, N_GROUPS * 3 * Hd), D_MODEL ** -0.5),
        "attn_w_o": nrm(ks[18], (N_ATTN_LAYERS, Hd, D_MODEL), Hd ** -0.5),
        "rel_bias": nrm(ks[19], (NUM_BUCKETS, N_GROUPS * HEADS_PER_GROUP), 0.2),
        "mlp_w_in": nrm(ks[20], (DEPTH, D_MODEL, D_FF), D_MODEL ** -0.5),
        "mlp_w_out": nrm(ks[21], (DEPTH, D_FF, D_MODEL), 0.5 * D_FF ** -0.5),
    }


def reference(x_prompt, x_sample, state_conv, cache_kv_w128, cache_kv_w512, cache_kv_w2048,
              norm_mix, norm_mlp, norm_final,
              conv_w_pw1, conv_b_pw1, conv_w_dw, conv_b_dw, conv_ln_g, conv_ln_b, conv_w_pw2, conv_b_pw2,
              attn_w_qkv, attn_w_o, rel_bias, mlp_w_in, mlp_w_out):
    win_caches = (cache_kv_w128, cache_kv_w512, cache_kv_w2048)
    biases = [group_bias(rel_bias, g) for g in range(N_GROUPS)]
    xp, xs = x_prompt, x_sample
    conv_p, conv_s = [], []
    kv_p = [[] for _ in range(N_GROUPS)]
    kv_s = [[] for _ in range(N_GROUPS)]
    for i in range(DEPTH):
        j = i // N_MIXERS
        hp = rmsnorm(xp, norm_mix[i])
        hs = rmsnorm(xs, norm_mix[i])
        if i % N_MIXERS == 0:
            prm = (conv_w_pw1[j], conv_b_pw1[j], conv_w_dw[j], conv_b_dw[j],
                   conv_ln_g[j], conv_ln_b[j], conv_w_pw2[j], conv_b_pw2[j])
            hist_p = jnp.zeros((xp.shape[0], CONV_WIDTH - 1, D_CONV), xp.dtype)
            mp, sp = conformer_conv(hp, hist_p, *prm)
            ms, ss = conformer_conv(hs, state_conv[j], *prm)
            conv_p.append(sp)
            conv_s.append(ss)
        else:
            mp, kvp = dilated_attention(hp, attn_w_qkv[j], attn_w_o[j], biases, None)
            ms, kvs = dilated_attention(hs, attn_w_qkv[j], attn_w_o[j], biases,
                                        [c[j] for c in win_caches])
            for g in range(N_GROUPS):
                kv_p[g].append(kvp[g])
                kv_s[g].append(kvs[g])
        xp = xp + mp
        xs = xs + ms
        xp = xp + sqrelu_mlp(rmsnorm(xp, norm_mlp[i]), mlp_w_in[i], mlp_w_out[i])
        xs = xs + sqrelu_mlp(rmsnorm(xs, norm_mlp[i]), mlp_w_in[i], mlp_w_out[i])
    y_prompt = rmsnorm(xp, norm_final)
    y_sample = rmsnorm(xs, norm_final)
    state_conv_prompt = jnp.stack(conv_p)
    state_conv_sample = jnp.stack(conv_s)
    kv_w128_prompt = jnp.stack(kv_p[0])
    kv_w128_sample = jnp.stack(kv_s[0])
    kv_w512_prompt = jnp.stack(kv_p[1])
    kv_w512_sample = jnp.stack(kv_s[1])
    kv_w2048_prompt = jnp.stack(kv_p[2])
    kv_w2048_sample = jnp.stack(kv_s[2])
    return (y_prompt, y_sample, state_conv_prompt, state_conv_sample,
            kv_w128_prompt, kv_w128_sample, kv_w512_prompt, kv_w512_sample,
            kv_w2048_prompt, kv_w2048_sample)
```

```python
import functools
import math

import jax
import jax.numpy as jnp
from jax import lax
from jax.experimental import pallas as pl
from jax.experimental.pallas import tpu as pltpu

F32 = jnp.float32
BF16 = jnp.bfloat16

D_MODEL = 1024
DEPTH = 4
HEAD_DIM = 64
N_HEADS = D_MODEL // HEAD_DIM
WINDOWS = (128, 512, 2048)
DILATIONS = (1, 4, 16)
N_GROUPS = 3
BLK = 128
SCALE = HEAD_DIM ** -0.5
NUM_BUCKETS = 32
MAX_DISTANCE = 2048
CONV_WIDTH = 31
HIST = CONV_WIDTH - 1
D_FF = 4 * D_MODEL
EPS = 1e-6
NEG = -1e30

LANES = 128
HALO = 32
MIB = 1 << 20

assert all(w // d == BLK for w, d in zip(WINDOWS, DILATIONS))


def _resident(shape):
    nd = len(shape)
    return pl.BlockSpec(shape, lambda *_: (0,) * nd, pipeline_mode=pl.Buffered(1))


def _cparams(n_axes, vmem_mib):
    return pltpu.CompilerParams(dimension_semantics=("arbitrary",) * n_axes,
                                vmem_limit_bytes=vmem_mib * MIB)


def _dot(a, b):
    return jnp.dot(a, b, preferred_element_type=F32)


def _dot_nt(a, b):
    return lax.dot_general(a, b, (((1,), (1,)), ((), ())), preferred_element_type=F32)


def _split_dot(a, b):
    hi = a.astype(BF16)
    lo = (a - hi.astype(F32)).astype(BF16)
    return _dot(hi, b) + _dot(lo, b)


def _rms(x, g):
    return x * lax.rsqrt(jnp.mean(x * x, axis=-1, keepdims=True) + EPS) * g


def _sigmoid(x):
    return 1.0 / (1.0 + jnp.exp(-x))


def _row_tile(m):
    return 512 if m % 512 == 0 else m


def _pw1_kernel(x_ref, g_ref, w_ref, b_ref, a_ref):
    h = _rms(x_ref[...], g_ref[...]).astype(BF16)
    u = _dot(h, w_ref[...]) + b_ref[...]
    a_ref[...] = u[:, :D_MODEL] * _sigmoid(u[:, D_MODEL:])


def _pw1_glu(x, g, w, b):
    m = x.shape[0]
    tm = _row_tile(m)
    return pl.pallas_call(
        _pw1_kernel,
        grid=(m // tm,),
        in_specs=[pl.BlockSpec((tm, D_MODEL), lambda i: (i, 0)),
                  _resident((1, D_MODEL)),
                  _resident((D_MODEL, 2 * D_MODEL)),
                  _resident((1, 2 * D_MODEL))],
        out_specs=pl.BlockSpec((tm, D_MODEL), lambda i: (i, 0)),
        out_shape=jax.ShapeDtypeStruct((m, D_MODEL), F32),
        compiler_params=_cparams(1, 40),
    )(x, g, w, b)


CONV_TS = 256
CONV_RC = 64
CONV_CC = 256


def _ln_swish_pw2(c, lng, lnb, w2, b2):
    mu = jnp.mean(c, axis=-1, keepdims=True)
    cc = c - mu
    var = jnp.mean(cc * cc, axis=-1, keepdims=True)
    z = cc * lax.rsqrt(var + EPS) * lng + lnb
    z = z * _sigmoid(z)
    return _dot(z.astype(BF16), w2) + b2


def _conv_prompt_kernel(a_ref, halo_ref, x_ref, wdw_ref, bdw_ref, lng_ref, lnb_ref,
                        w2_ref, b2_ref, o_ref, full_ref, c_ref):
    i = pl.program_id(1)
    full_ref[0:HALO, :] = jnp.where(i > 0, halo_ref[...], 0.0)
    full_ref[HALO:, :] = a_ref[...]
    off = HALO - HIST
    for r0 in range(0, CONV_TS, CONV_RC):
        for c0 in range(0, D_MODEL, CONV_CC):
            acc = jnp.broadcast_to(bdw_ref[:, c0:c0 + CONV_CC], (CONV_RC, CONV_CC))
            for k in range(CONV_WIDTH):
                acc = acc + (wdw_ref[k:k + 1, c0:c0 + CONV_CC]
                             * full_ref[r0 + off + k:r0 + off + k + CONV_RC, c0:c0 + CONV_CC])
            c_ref[r0:r0 + CONV_RC, c0:c0 + CONV_CC] = acc
    y = _ln_swish_pw2(c_ref[...], lng_ref[...], lnb_ref[...], w2_ref[...], b2_ref[...])
    o_ref[...] = x_ref[...] + y


def _conv_prompt(a, x, batch, seq, wdw, bdw, lng, lnb, w2, b2):
    m = batch * seq
    nt = seq // CONV_TS
    hpt = CONV_TS // HALO
    row = lambda b, i: (b * nt + i, 0)
    return pl.pallas_call(
        _conv_prompt_kernel,
        grid=(batch, nt),
        in_specs=[pl.BlockSpec((CONV_TS, D_MODEL), row),
                  pl.BlockSpec((HALO, D_MODEL),
                               lambda b, i: (jnp.maximum((b * nt + i) * hpt - 1, 0), 0)),
                  pl.BlockSpec((CONV_TS, D_MODEL), row),
                  _resident((CONV_WIDTH, D_MODEL)),
                  _resident((1, D_MODEL)), _resident((1, D_MODEL)), _resident((1, D_MODEL)),
                  _resident((D_MODEL, D_MODEL)), _resident((1, D_MODEL))],
        out_specs=pl.BlockSpec((CONV_TS, D_MODEL), row),
        out_shape=jax.ShapeDtypeStruct((m, D_MODEL), F32),
        scratch_shapes=[pltpu.VMEM((CONV_TS + HALO, D_MODEL), F32),
                        pltpu.VMEM((CONV_TS, D_MODEL), F32)],
        compiler_params=_cparams(2, 32),
    )(a, a, x, wdw, bdw, lng, lnb, w2, b2)


CONV_SB = 32


def _conv_sample_kernel(a_ref, st_ref, x_ref, wdw_ref, bdw_ref, lng_ref, lnb_ref,
                        w2_ref, b2_ref, o_ref, ns_ref):
    a = a_ref[...]
    c = a * wdw_ref[HIST:CONV_WIDTH, :] + bdw_ref[...]
    for t in range(HIST):
        st = st_ref[t]
        c = c + st * wdw_ref[t:t + 1, :]
        if t > 0:
            ns_ref[t - 1] = st
    ns_ref[HIST - 1] = a
    y = _ln_swish_pw2(c, lng_ref[...], lnb_ref[...], w2_ref[...], b2_ref[...])
    o_ref[...] = x_ref[...] + y


def _conv_sample(a, state_t, layer, x, wdw, bdw, lng, lnb, w2, b2):
    db = x.shape[0]
    row = lambda i: (i, 0)
    return pl.pallas_call(
        _conv_sample_kernel,
        grid=(db // CONV_SB,),
        in_specs=[pl.BlockSpec((CONV_SB, D_MODEL), row),
                  pl.BlockSpec((None, HIST, CONV_SB, D_MODEL), lambda i: (layer, 0, i, 0)),
                  pl.BlockSpec((CONV_SB, D_MODEL), row),
                  _resident((CONV_WIDTH, D_MODEL)),
                  _resident((1, D_MODEL)), _resident((1, D_MODEL)), _resident((1, D_MODEL)),
                  _resident((D_MODEL, D_MODEL)), _resident((1, D_MODEL))],
        out_specs=[pl.BlockSpec((CONV_SB, D_MODEL), row),
                   pl.BlockSpec((HIST, CONV_SB, D_MODEL), lambda i: (0, i, 0))],
        out_shape=[jax.ShapeDtypeStruct((db, D_MODEL), F32),
                   jax.ShapeDtypeStruct((HIST, db, D_MODEL), F32)],
        compiler_params=_cparams(1, 40),
    )(a, state_t, x, wdw, bdw, lng, lnb, w2, b2)


MLP_FC = 1024


def _mlp_kernel(x_ref, g_ref, win_ref, wout_ref, o_ref):
    x = x_ref[...]
    h = _rms(x, g_ref[...]).astype(BF16)
    acc = x
    for c0 in range(0, D_FF, MLP_FC):
        u = jnp.maximum(_dot(h, win_ref[:, c0:c0 + MLP_FC]), 0.0)
        acc = acc + _dot((u * u).astype(BF16), wout_ref[c0:c0 + MLP_FC, :])
    o_ref[...] = acc


def _mlp(x, g, w_in, w_out):
    m = x.shape[0]
    tm = _row_tile(m)
    return pl.pallas_call(
        _mlp_kernel,
        grid=(m // tm,),
        in_specs=[pl.BlockSpec((tm, D_MODEL), lambda i: (i, 0)),
                  _resident((1, D_MODEL)),
                  _resident((D_MODEL, D_FF)),
                  _resident((D_FF, D_MODEL))],
        out_specs=pl.BlockSpec((tm, D_MODEL), lambda i: (i, 0)),
        out_shape=jax.ShapeDtypeStruct((m, D_MODEL), F32),
        compiler_params=_cparams(1, 48),
    )(x, g, w_in, w_out)


def _qkv_kernel(keep_rows, x_ref, g_ref, w_ref, qkv_ref, *kv_ref):
    h = _rms(x_ref[...], g_ref[...]).astype(BF16)
    tm = x_ref.shape[0]
    for c in range(3):
        r = _dot(h, w_ref[:, c * D_MODEL:(c + 1) * D_MODEL])
        if c == 0:
            r = r * SCALE
        qkv_ref[:, c * D_MODEL:(c + 1) * D_MODEL] = r.astype(qkv_ref.dtype)
        if kv_ref and c > 0:
            kv_ref[0][0, :, (c - 1) * D_MODEL:c * D_MODEL] = r[tm - keep_rows:, :]


def _qkv_proj(x, g, w, *, out_dtype, batch=None, keep=None):
    m = x.shape[0]
    tm = _row_tile(m)
    in_specs = [pl.BlockSpec((tm, D_MODEL), lambda i: (i, 0)),
                _resident((1, D_MODEL)),
                _resident((D_MODEL, 3 * D_MODEL))]
    out_specs = [pl.BlockSpec((tm, 3 * D_MODEL), lambda i: (i, 0))]
    out_shape = [jax.ShapeDtypeStruct((m, 3 * D_MODEL), out_dtype)]
    keep_rows = 0
    if keep is not None:
        tpb = m // batch // tm
        keep_rows = min(keep, tm)
        first = tpb - keep // keep_rows
        out_specs.append(pl.BlockSpec(
            (1, keep_rows, 2 * D_MODEL),
            lambda i: (i // tpb, jnp.maximum(i % tpb - first, 0), 0)))
        out_shape.append(jax.ShapeDtypeStruct((batch, keep, 2 * D_MODEL), F32))
    return pl.pallas_call(
        functools.partial(_qkv_kernel, keep_rows),
        grid=(m // tm,),
        in_specs=in_specs, out_specs=out_specs, out_shape=out_shape,
        compiler_params=_cparams(1, 48),
    )(x, g, w)


def _attn_prompt_kernel(has_prev, q_ref, kc_ref, vc_ref, *rest):
    if has_prev:
        kp_ref, vp_ref, bc_ref, bp_ref, o_ref, lse_ref = rest
        prev_off = jnp.where(pl.program_id(2) == 0, NEG, 0.0).astype(F32)
    else:
        bc_ref, o_ref, lse_ref = rest
    lane = lax.broadcasted_iota(jnp.int32, (BLK, LANES), 1)
    lse_all = jnp.zeros((BLK, LANES), F32)
    for h in range(N_HEADS):
        hs = slice(h * HEAD_DIM, (h + 1) * HEAD_DIM)
        qh = q_ref[:, hs]
        s_c = _dot_nt(qh, kc_ref[:, hs]) + bc_ref[h]
        m = jnp.max(s_c, axis=-1, keepdims=True)
        if has_prev:
            s_p = _dot_nt(qh, kp_ref[:, hs]) + bp_ref[h] + prev_off
            m = jnp.maximum(m, jnp.max(s_p, axis=-1, keepdims=True))
        p_c = jnp.exp(s_c - m)
        den = jnp.sum(p_c, axis=-1, keepdims=True)
        o = _dot(p_c.astype(BF16), vc_ref[:, hs])
        if has_prev:
            p_p = jnp.exp(s_p - m)
            den = den + jnp.sum(p_p, axis=-1, keepdims=True)
            o = o + _dot(p_p.astype(BF16), vp_ref[:, hs])
        o_ref[:, hs] = o / den
        lse_all = jnp.where(lane == h, m + jnp.log(den), lse_all)
    lse_ref[...] = lse_all


def _attn_prompt(qkv, g, batch, seq, bias_cur, bias_prev):
    d = DILATIONS[g]
    m = batch * seq
    nb = seq // d // BLK
    has_prev = nb > 1
    qkv_v = qkv.reshape(m // d, d * 3 * D_MODEL)
    blk = (BLK, D_MODEL)

    def cur(c):
        return pl.BlockSpec(blk, lambda b, r, n: (b * nb + n, r * 3 + c))

    def prev(c):
        return pl.BlockSpec(blk, lambda b, r, n: (b * nb + jnp.maximum(n - 1, 0), r * 3 + c))

    tbl = _resident((N_HEADS, BLK, BLK))
    in_specs = [cur(0), cur(1), cur(2)]
    args = [qkv_v, qkv_v, qkv_v]
    if has_prev:
        in_specs += [prev(1), prev(2), tbl, tbl]
        args += [qkv_v, qkv_v, bias_cur, bias_prev]
    else:
        in_specs += [tbl]
        args += [bias_cur]
    o, lse = pl.pallas_call(
        functools.partial(_attn_prompt_kernel, has_prev),
        grid=(batch, d, nb),
        in_specs=in_specs,
        out_specs=[pl.BlockSpec(blk, lambda b, r, n: (b * nb + n, r)),
                   pl.BlockSpec((BLK, LANES), lambda b, r, n: (b * nb + n, r))],
        out_shape=[jax.ShapeDtypeStruct((m // d, d * D_MODEL), F32),
                   jax.ShapeDtypeStruct((m // d, d * LANES), F32)],
        compiler_params=_cparams(3, 32),
    )(*args)
    return o.reshape(m, D_MODEL), lse.reshape(m, LANES)


def _merge_proj_kernel(o0_ref, o1_ref, o2_ref, l0_ref, l1_ref, l2_ref, x_ref, exp_ref, wo_ref, out_ref):
    l0, l1, l2 = l0_ref[...], l1_ref[...], l2_ref[...]
    mx = jnp.maximum(jnp.maximum(l0, l1), l2)
    e0, e1, e2 = jnp.exp(l0 - mx), jnp.exp(l1 - mx), jnp.exp(l2 - mx)
    tot = e0 + e1 + e2
    ex = exp_ref[...]
    o = (_split_dot(e0 / tot, ex) * o0_ref[...]
         + _split_dot(e1 / tot, ex) * o1_ref[...]
         + _split_dot(e2 / tot, ex) * o2_ref[...])
    out_ref[...] = x_ref[...] + _dot(o.astype(BF16), wo_ref[...])


def _merge_proj(outs, lses, x, head_expand, w_o):
    m = x.shape[0]
    tm = _row_tile(m)
    row = pl.BlockSpec((tm, D_MODEL), lambda i: (i, 0))
    lrow = pl.BlockSpec((tm, LANES), lambda i: (i, 0))
    return pl.pallas_call(
        _merge_proj_kernel,
        grid=(m // tm,),
        in_specs=[row, row, row, lrow, lrow, lrow, row,
                  _resident((LANES, D_MODEL)), _resident((D_MODEL, D_MODEL))],
        out_specs=row,
        out_shape=jax.ShapeDtypeStruct((m, D_MODEL), F32),
        compiler_params=_cparams(1, 48),
    )(*outs, *lses, x, head_expand, w_o)


ATT_HB = 4
ATT_PAIRS = ATT_HB // 2


def _attn_sample_kernel(q0_ref, q1_ref, q2_ref, c0_ref, c1_ref, c2_ref, t0_ref, t1_ref, t2_ref,
                        tn_ref, o_ref):
    lane = lax.broadcasted_iota(jnp.int32, (HEAD_DIM, LANES), 1)
    pad = jnp.zeros((LANES - 3 * ATT_PAIRS, LANES), F32)
    qts = [jnp.concatenate([q_ref[...], pad], axis=0).T for q_ref in (q0_ref, q1_ref, q2_ref)]
    halves = [jnp.zeros((HEAD_DIM, LANES), F32) for _ in range(2)]
    for hl in range(ATT_HB):
        rows = slice((hl % 2) * HEAD_DIM, (hl % 2 + 1) * HEAD_DIM)
        uns, lses, dens = [], [], []
        for g, (c_ref, t_ref) in enumerate(((c0_ref, t0_ref), (c1_ref, t1_ref), (c2_ref, t2_ref))):
            def col(c, qt=qts[g]):
                j = c * ATT_PAIRS + hl // 2
                return jnp.broadcast_to(qt[rows, j:j + 1], (HEAD_DIM, LANES))
            qb, knb, vnb = col(0), col(1), col(2)
            s_n = jnp.sum(qb * knb, axis=0, keepdims=True) + tn_ref[g, hl:hl + 1, :]
            chunks = [slice(k, k + LANES) for k in range(0, c_ref.shape[-1], LANES)]
            s = [jnp.sum(c_ref[0, hl, :, ck] * qb, axis=0, keepdims=True) + t_ref[hl:hl + 1, ck]
                 for ck in chunks]
            mx = functools.reduce(jnp.maximum, s)
            mx = jnp.maximum(jnp.max(mx, axis=-1, keepdims=True), s_n)
            p = [jnp.exp(sc - mx) for sc in s]
            p_n = jnp.exp(s_n - mx)
            den = jnp.sum(functools.reduce(jnp.add, p), axis=-1, keepdims=True) + p_n
            acc = functools.reduce(jnp.add, [c_ref[1, hl, :, ck] * pc for ck, pc in zip(chunks, p)])
            uns.append(jnp.sum(acc, axis=-1, keepdims=True) + p_n * vnb)
            lses.append(mx + jnp.log(den))
            dens.append(den)
        lmx = jnp.maximum(jnp.maximum(lses[0], lses[1]), lses[2])
        es = [jnp.exp(l - lmx) for l in lses]
        tot = es[0] + es[1] + es[2]
        o = functools.reduce(jnp.add, [(e / (tot * dn)) * un for e, dn, un in zip(es, dens, uns)])
        halves[hl % 2] = jnp.where(lane == hl // 2, o, halves[hl % 2])
    o_ref[...] = jnp.concatenate(halves, axis=0).T[0:ATT_PAIRS, :]


def _attn_sample(qkvs, caches_t, layer, tabs, tab_new):
    db = qkvs[0].shape[0]
    nhb = N_HEADS // ATT_HB
    qs, cspecs, tspecs, tabs_v = [], [], [], []
    for g in range(N_GROUPS):
        w = WINDOWS[g]
        q = qkvs[g].reshape(db, 3, nhb, ATT_PAIRS, LANES).transpose(0, 2, 1, 3, 4)
        qs.append(q.reshape(db, nhb, 3 * ATT_PAIRS, LANES))
        cspecs.append(pl.BlockSpec((None, None, 2, ATT_HB, HEAD_DIM, w),
                                   lambda b, hb: (layer, b, 0, hb, 0, 0)))
        tabs_v.append(tabs[g].reshape(nhb, ATT_HB, w))
        tspecs.append(pl.BlockSpec((None, ATT_HB, w), lambda b, hb: (hb, 0, 0)))
    qspec = pl.BlockSpec((None, None, 3 * ATT_PAIRS, LANES), lambda b, hb: (b, hb, 0, 0))
    o = pl.pallas_call(
        _attn_sample_kernel,
        grid=(db, nhb),
        in_specs=[qspec, qspec, qspec, *cspecs, *tspecs,
                  pl.BlockSpec((N_GROUPS, None, ATT_HB, LANES), lambda b, hb: (0, hb, 0, 0))],
        out_specs=pl.BlockSpec((None, None, ATT_PAIRS, LANES), lambda b, hb: (b, hb, 0, 0)),
        out_shape=jax.ShapeDtypeStruct((db, nhb, ATT_PAIRS, LANES), F32),
        compiler_params=_cparams(2, 32),
    )(*qs, *caches_t, *tabs_v, tab_new.reshape(N_GROUPS, nhb, ATT_HB, LANES))
    return o.reshape(db, D_MODEL)


def _proj_kernel(o_ref, x_ref, w_ref, out_ref):
    out_ref[...] = x_ref[...] + _dot(o_ref[...].astype(BF16), w_ref[...])


def _proj_residual(o, x, w):
    m = x.shape[0]
    tm = _row_tile(m)
    row = pl.BlockSpec((tm, D_MODEL), lambda i: (i, 0))
    return pl.pallas_call(
        _proj_kernel,
        grid=(m // tm,),
        in_specs=[row, row, _resident((D_MODEL, D_MODEL))],
        out_specs=row,
        out_shape=jax.ShapeDtypeStruct((m, D_MODEL), F32),
        compiler_params=_cparams(1, 16),
    )(o, x, w)


def _norm_kernel(x_ref, g_ref, o_ref):
    o_ref[...] = _rms(x_ref[...], g_ref[...])


def _final_norm(x, g):
    m = x.shape[0]
    tm = _row_tile(m)
    return pl.pallas_call(
        _norm_kernel,
        grid=(m // tm,),
        in_specs=[pl.BlockSpec((tm, D_MODEL), lambda i: (i, 0)), _resident((1, D_MODEL))],
        out_specs=pl.BlockSpec((tm, D_MODEL), lambda i: (i, 0)),
        out_shape=jax.ShapeDtypeStruct((m, D_MODEL), F32),
        compiler_params=_cparams(1, 16),
    )(x, g)


def _t5_bucket(dist):
    max_exact = NUM_BUCKETS // 2
    n = jnp.maximum(dist.astype(F32), 1.0)
    large = max_exact + (jnp.log(n / max_exact) / math.log(MAX_DISTANCE / max_exact)
                         * (NUM_BUCKETS - max_exact)).astype(jnp.int32)
    large = jnp.minimum(large, NUM_BUCKETS - 1)
    return jnp.where(dist < max_exact, dist, large)


def _bias_tables(rel_bias):
    qi = jnp.arange(BLK)[:, None]
    kj = jnp.arange(BLK)[None, :]
    d_cur = qi - kj
    d_prev = qi + BLK - kj
    cur, prev, tab, tab_new = [], [], [], []
    for g in range(N_GROUPS):
        dist = jnp.arange(BLK + 1, dtype=jnp.int32) * DILATIONS[g]
        tbl = rel_bias[:, g * N_HEADS:(g + 1) * N_HEADS].astype(F32)
        bias = tbl[_t5_bucket(dist)].T
        cur.append(jnp.where(d_cur >= 0, bias[:, jnp.clip(d_cur, 0, BLK)], NEG))
        prev.append(jnp.where(d_prev <= BLK, bias[:, jnp.clip(d_prev, 0, BLK)], NEG))
        w = jnp.arange(WINDOWS[g])
        off = (WINDOWS[g] - w) // DILATIONS[g]
        tab.append(jnp.where(w % DILATIONS[g] == 0, bias[:, off], NEG))
        tab_new.append(jnp.broadcast_to(bias[:, 0:1], (N_HEADS, LANES)))
    return cur, prev, tab, jnp.stack(tab_new)


def kernel(x_prompt, x_sample, state_conv, cache_kv_w128, cache_kv_w512, cache_kv_w2048,
           norm_mix, norm_mlp, norm_final,
           conv_w_pw1, conv_b_pw1, conv_w_dw, conv_b_dw, conv_ln_g, conv_ln_b, conv_w_pw2, conv_b_pw2,
           attn_w_qkv, attn_w_o, rel_bias, mlp_w_in, mlp_w_out):
    batch, seq, _ = x_prompt.shape
    db = x_sample.shape[0]
    assert x_sample.shape[1] == 1 and seq % (DILATIONS[-1] * BLK) == 0
    caches = (cache_kv_w128, cache_kv_w512, cache_kv_w2048)
    assert all(c.shape[2] == w for c, w in zip(caches, WINDOWS))

    xp = x_prompt.reshape(batch * seq, D_MODEL)
    xs = x_sample.reshape(db, D_MODEL)
    row = lambda v: v.reshape(1, -1)

    bias_cur, bias_prev, tabs, tab_new = _bias_tables(rel_bias)
    head_expand = (jnp.arange(LANES)[:, None] == jnp.arange(D_MODEL)[None, :] // HEAD_DIM).astype(BF16)
    caches_t = [jnp.transpose(c, (0, 1, 3, 4, 5, 2)) for c in caches]
    state_t = jnp.transpose(state_conv, (0, 2, 1, 3))

    conv_p, conv_s = [], []
    kv_p = [[] for _ in range(N_GROUPS)]
    kv_s = [[] for _ in range(N_GROUPS)]
    for i in range(DEPTH):
        j = i // 2
        g_mix = row(norm_mix[i])
        if i % 2 == 0:
            w1 = conv_w_pw1[j].astype(BF16)
            w2 = conv_w_pw2[j].astype(BF16)
            prm = (conv_w_dw[j], row(conv_b_dw[j]), row(conv_ln_g[j]), row(conv_ln_b[j]),
                   w2, row(conv_b_pw2[j]))
            a_p = _pw1_glu(xp, g_mix, w1, row(conv_b_pw1[j]))
            a_s = _pw1_glu(xs, g_mix, w1, row(conv_b_pw1[j]))
            xp = _conv_prompt(a_p, xp, batch, seq, *prm)
            xs, ns = _conv_sample(a_s, state_t, j, xs, *prm)
            conv_p.append(a_p.reshape(batch, seq, D_MODEL)[:, seq - HIST:])
            conv_s.append(jnp.transpose(ns, (1, 0, 2)))
        else:
            w_qkv = attn_w_qkv[j].astype(BF16)
            w_o = attn_w_o[j].astype(BF16)
            outs, lses, qkv_s = [], [], []
            for g in range(N_GROUPS):
                w_g = w_qkv[:, g * 3 * D_MODEL:(g + 1) * 3 * D_MODEL]
                keep = min(WINDOWS[g], seq)
                qkv, kv = _qkv_proj(xp, g_mix, w_g, out_dtype=BF16, batch=batch, keep=keep)
                kv_p[g].append(kv.reshape(batch, keep, 2, N_HEADS, HEAD_DIM))
                o, lse = _attn_prompt(qkv, g, batch, seq, bias_cur[g], bias_prev[g])
                outs.append(o)
                lses.append(lse)
                (q_s,) = _qkv_proj(xs, g_mix, w_g, out_dtype=F32)
                qkv_s.append(q_s)
                kv_s[g].append(q_s[:, D_MODEL:].reshape(db, 1, 2, N_HEADS, HEAD_DIM))
            xp = _merge_proj(outs, lses, xp, head_expand, w_o)
            o_s = _attn_sample(qkv_s, caches_t, j, tabs, tab_new)
            xs = _proj_residual(o_s, xs, w_o)
        g_mlp = row(norm_mlp[i])
        w_in = mlp_w_in[i].astype(BF16)
        w_out = mlp_w_out[i].astype(BF16)
        xp = _mlp(xp, g_mlp, w_in, w_out)
        xs = _mlp(xs, g_mlp, w_in, w_out)

    y_prompt = _final_norm(xp, row(norm_final)).reshape(batch, seq, D_MODEL)
    y_sample = _final_norm(xs, row(norm_final)).reshape(db, 1, D_MODEL)
    return (y_prompt, y_sample, jnp.stack(conv_p), jnp.stack(conv_s),
            jnp.stack(kv_p[0]), jnp.stack(kv_s[0]), jnp.stack(kv_p[1]), jnp.stack(kv_s[1]),
            jnp.stack(kv_p[2]), jnp.stack(kv_s[2]))
```

```python
import functools
import math

import jax
import jax.numpy as jnp
from jax import lax
from jax.experimental import pallas as pl
from jax.experimental.pallas import tpu as pltpu

F32 = jnp.float32
BF16 = jnp.bfloat16

D_MODEL = 1024
DEPTH = 4
HEAD_DIM = 64
N_HEADS = D_MODEL // HEAD_DIM
WINDOWS = (128, 512, 2048)
DILATIONS = (1, 4, 16)
N_GROUPS = 3
BLK = 128
SCALE = HEAD_DIM ** -0.5
NUM_BUCKETS = 32
MAX_DISTANCE = 2048
CONV_WIDTH = 31
HIST = CONV_WIDTH - 1
D_FF = 4 * D_MODEL
EPS = 1e-6
NEG = -1e30

LANES = 128
N_PAIRS = D_MODEL // LANES
HALO = 32
MIB = 1 << 20

assert all(w // d == BLK for w, d in zip(WINDOWS, DILATIONS))
assert 2 * HEAD_DIM == LANES


def _resident(shape):
    nd = len(shape)
    return pl.BlockSpec(shape, lambda *_: (0,) * nd, pipeline_mode=pl.Buffered(1))


def _cparams(n_axes, vmem_mib):
    return pltpu.CompilerParams(dimension_semantics=("arbitrary",) * n_axes,
                                vmem_limit_bytes=vmem_mib * MIB)


def _dot(a, b):
    return jnp.dot(a, b, preferred_element_type=F32)


def _dot_nt(a, b):
    return lax.dot_general(a, b, (((1,), (1,)), ((), ())), preferred_element_type=F32)


def _rms(x, g):
    return x * lax.rsqrt(jnp.mean(x * x, axis=-1, keepdims=True) + EPS) * g


def _sigmoid(x):
    return 1.0 / (1.0 + jnp.exp(-x))


def _row_tile(m, tile=512):
    return tile if m % tile == 0 else m


def _pw1_kernel(x_ref, g_ref, w_ref, b_ref, a_ref):
    h = _rms(x_ref[...], g_ref[...]).astype(BF16)
    u = _dot(h, w_ref[...]) + b_ref[...]
    a_ref[...] = u[:, :D_MODEL] * _sigmoid(u[:, D_MODEL:])


def _pw1_glu(x, g, w, b):
    m = x.shape[0]
    tm = _row_tile(m)
    return pl.pallas_call(
        _pw1_kernel,
        grid=(m // tm,),
        in_specs=[pl.BlockSpec((tm, D_MODEL), lambda i: (i, 0)),
                  _resident((1, D_MODEL)),
                  _resident((D_MODEL, 2 * D_MODEL)),
                  _resident((1, 2 * D_MODEL))],
        out_specs=pl.BlockSpec((tm, D_MODEL), lambda i: (i, 0)),
        out_shape=jax.ShapeDtypeStruct((m, D_MODEL), F32),
        compiler_params=_cparams(1, 40),
        name="pw1_glu",
    )(x, g, w, b)


CONV_TS = 256
CONV_RC = 64
CONV_CC = 256


def _ln_swish_pw2(c, lng, lnb, w2, b2):
    mu = jnp.mean(c, axis=-1, keepdims=True)
    cc = c - mu
    var = jnp.mean(cc * cc, axis=-1, keepdims=True)
    z = cc * lax.rsqrt(var + EPS) * lng + lnb
    z = z * _sigmoid(z)
    return _dot(z.astype(BF16), w2) + b2


def _conv_prompt_kernel(a_ref, halo_ref, x_ref, wdw_ref, bdw_ref, lng_ref, lnb_ref,
                        w2_ref, b2_ref, o_ref, full_ref, c_ref):
    i = pl.program_id(1)
    full_ref[0:HALO, :] = jnp.where(i > 0, halo_ref[...], 0.0)
    full_ref[HALO:, :] = a_ref[...]
    off = HALO - HIST
    for r0 in range(0, CONV_TS, CONV_RC):
        for c0 in range(0, D_MODEL, CONV_CC):
            acc = jnp.broadcast_to(bdw_ref[:, c0:c0 + CONV_CC], (CONV_RC, CONV_CC))
            for k in range(CONV_WIDTH):
                acc = acc + (wdw_ref[k:k + 1, c0:c0 + CONV_CC]
                             * full_ref[r0 + off + k:r0 + off + k + CONV_RC, c0:c0 + CONV_CC])
            c_ref[r0:r0 + CONV_RC, c0:c0 + CONV_CC] = acc
    y = _ln_swish_pw2(c_ref[...], lng_ref[...], lnb_ref[...], w2_ref[...], b2_ref[...])
    o_ref[...] = x_ref[...] + y


def _conv_prompt(a, x, batch, seq, wdw, bdw, lng, lnb, w2, b2):
    m = batch * seq
    nt = seq // CONV_TS
    hpt = CONV_TS // HALO
    row = lambda b, i: (b * nt + i, 0)
    return pl.pallas_call(
        _conv_prompt_kernel,
        grid=(batch, nt),
        in_specs=[pl.BlockSpec((CONV_TS, D_MODEL), row),
                  pl.BlockSpec((HALO, D_MODEL),
                               lambda b, i: (jnp.maximum((b * nt + i) * hpt - 1, 0), 0)),
                  pl.BlockSpec((CONV_TS, D_MODEL), row),
                  _resident((CONV_WIDTH, D_MODEL)),
                  _resident((1, D_MODEL)), _resident((1, D_MODEL)), _resident((1, D_MODEL)),
                  _resident((D_MODEL, D_MODEL)), _resident((1, D_MODEL))],
        out_specs=pl.BlockSpec((CONV_TS, D_MODEL), row),
        out_shape=jax.ShapeDtypeStruct((m, D_MODEL), F32),
        scratch_shapes=[pltpu.VMEM((CONV_TS + HALO, D_MODEL), F32),
                        pltpu.VMEM((CONV_TS, D_MODEL), F32)],
        compiler_params=_cparams(2, 32),
        name="conv_prompt",
    )(a, a, x, wdw, bdw, lng, lnb, w2, b2)


CONV_SB = 32


def _conv_sample_kernel(a_ref, st_ref, x_ref, wdw_ref, bdw_ref, lng_ref, lnb_ref,
                        w2_ref, b2_ref, o_ref, ns_ref):
    a = a_ref[...]
    c = a * wdw_ref[HIST:CONV_WIDTH, :] + bdw_ref[...]
    for t in range(HIST):
        st = st_ref[t]
        c = c + st * wdw_ref[t:t + 1, :]
        if t > 0:
            ns_ref[t - 1] = st
    ns_ref[HIST - 1] = a
    y = _ln_swish_pw2(c, lng_ref[...], lnb_ref[...], w2_ref[...], b2_ref[...])
    o_ref[...] = x_ref[...] + y


def _conv_sample(a, state_t, layer, x, wdw, bdw, lng, lnb, w2, b2):
    db = x.shape[0]
    row = lambda i: (i, 0)
    return pl.pallas_call(
        _conv_sample_kernel,
        grid=(db // CONV_SB,),
        in_specs=[pl.BlockSpec((CONV_SB, D_MODEL), row),
                  pl.BlockSpec((None, HIST, CONV_SB, D_MODEL), lambda i: (layer, 0, i, 0)),
                  pl.BlockSpec((CONV_SB, D_MODEL), row),
                  _resident((CONV_WIDTH, D_MODEL)),
                  _resident((1, D_MODEL)), _resident((1, D_MODEL)), _resident((1, D_MODEL)),
                  _resident((D_MODEL, D_MODEL)), _resident((1, D_MODEL))],
        out_specs=[pl.BlockSpec((CONV_SB, D_MODEL), row),
                   pl.BlockSpec((HIST, CONV_SB, D_MODEL), lambda i: (0, i, 0))],
        out_shape=[jax.ShapeDtypeStruct((db, D_MODEL), F32),
                   jax.ShapeDtypeStruct((HIST, db, D_MODEL), F32)],
        compiler_params=_cparams(1, 40),
        name="conv_sample",
    )(a, state_t, x, wdw, bdw, lng, lnb, w2, b2)


MLP_FC = 1024


def _mlp_kernel(x_ref, g_ref, win_ref, wout_ref, o_ref):
    x = x_ref[...]
    h = _rms(x, g_ref[...]).astype(BF16)
    acc = x
    for c0 in range(0, D_FF, MLP_FC):
        u = jnp.maximum(_dot(h, win_ref[:, c0:c0 + MLP_FC]), 0.0)
        acc = acc + _dot((u * u).astype(BF16), wout_ref[c0:c0 + MLP_FC, :])
    o_ref[...] = acc


def _mlp(x, g, w_in, w_out):
    m = x.shape[0]
    tm = _row_tile(m)
    return pl.pallas_call(
        _mlp_kernel,
        grid=(m // tm,),
        in_specs=[pl.BlockSpec((tm, D_MODEL), lambda i: (i, 0)),
                  _resident((1, D_MODEL)),
                  _resident((D_MODEL, D_FF)),
                  _resident((D_FF, D_MODEL))],
        out_specs=pl.BlockSpec((tm, D_MODEL), lambda i: (i, 0)),
        out_shape=jax.ShapeDtypeStruct((m, D_MODEL), F32),
        compiler_params=_cparams(1, 48),
        name="mlp",
    )(x, g, w_in, w_out)


QKV_TM = 256
QKV_COLS = N_GROUPS * 3 * D_MODEL


def _qkv_kernel(keep_rows, x_ref, g_ref, w_ref, qkv_ref, *kv_refs):
    h = _rms(x_ref[...], g_ref[...]).astype(BF16)
    tm = x_ref.shape[0]
    for g in range(N_GROUPS):
        for c in range(3):
            col = (g * 3 + c) * D_MODEL
            r = _dot(h, w_ref[:, col:col + D_MODEL])
            if c == 0:
                r = r * SCALE
            qkv_ref[:, col:col + D_MODEL] = r.astype(qkv_ref.dtype)
            if kv_refs and c > 0:
                kv_refs[g][:, (c - 1) * D_MODEL:c * D_MODEL] = r[tm - keep_rows[g]:, :]


def _qkv_proj(x, g, w, *, batch=None, seq=None):
    m = x.shape[0]
    prompt = batch is not None
    tm = QKV_TM if prompt else m
    in_specs = [pl.BlockSpec((tm, D_MODEL), lambda i: (i, 0)),
                _resident((1, D_MODEL)),
                _resident((D_MODEL, QKV_COLS))]
    out_specs = [pl.BlockSpec((tm, QKV_COLS), lambda i: (i, 0))]
    out_shape = [jax.ShapeDtypeStruct((m, QKV_COLS), BF16 if prompt else F32)]
    keep_rows = ()
    if prompt:
        tpb = seq // tm
        for win in WINDOWS:
            keep = min(win, seq)
            rows = min(keep, tm)
            first = tpb - keep // rows
            keep_rows += (rows,)
            out_specs.append(pl.BlockSpec(
                (None, rows, 2 * D_MODEL),
                lambda i, first=first: (i // tpb, jnp.maximum(i % tpb - first, 0), 0)))
            out_shape.append(jax.ShapeDtypeStruct((batch, keep, 2 * D_MODEL), F32))
    return pl.pallas_call(
        functools.partial(_qkv_kernel, keep_rows),
        grid=(m // tm,),
        in_specs=in_specs, out_specs=out_specs, out_shape=out_shape,
        compiler_params=_cparams(1, 56),
        name="qkv_proj",
    )(x, g, w)


MERGE_ROWS = 256


def _attn_prompt_kernel(*refs):
    qkv_refs, (tab_ref, o_ref, f_scr, og_scr, lg_scr) = refs[:9], refs[9:]
    seq = o_ref.shape[0]
    lo_q = lax.broadcasted_iota(jnp.int32, (BLK, LANES), 1) < HEAD_DIM
    m_lo, m_hi = {}, {}
    for n_rows in (BLK, 2 * BLK):
        lo = lax.broadcasted_iota(jnp.int32, (n_rows, LANES), 1) < HEAD_DIM
        m_lo[n_rows] = jnp.where(lo, 1.0, 0.0).astype(BF16)
        m_hi[n_rows] = jnp.where(lo, 0.0, 1.0).astype(BF16)

    for g in range(N_GROUPS):
        d = DILATIONS[g]
        if d == 1:
            src = qkv_refs[0:3]
        else:
            src = []
            for c in range(3):
                f_scr[(g - 1) * 3 + c] = qkv_refs[g * 3 + c][...].astype(F32)
                src.append(f_scr.at[(g - 1) * 3 + c])

        def rows(start, d=d):
            return pl.ds(start, BLK) if d == 1 else pl.ds(start, BLK, stride=d)

        def load(c, start, src=src):
            return src[c][rows(start), :].astype(BF16)

        n_blk = seq // d // BLK
        for r in range(d):
            for n in range(n_blk):
                start = r + n * BLK * d
                q = load(0, start)
                kc, vc = load(1, start), load(2, start)
                if n == 0:
                    keys, vals, tab = kc, vc, tab_ref[g, :, BLK:]
                else:
                    prev = start - BLK * d
                    keys = jnp.concatenate([load(1, prev), kc], axis=0)
                    vals = jnp.concatenate([load(2, prev), vc], axis=0)
                    tab = tab_ref[g]
                nk = keys.shape[0]
                q2 = jnp.concatenate([q * m_lo[BLK], q * m_hi[BLK]], axis=0)
                s = _dot_nt(q2, keys) + tab
                mx = jnp.max(s, axis=-1, keepdims=True)
                p = jnp.exp(s - mx).astype(BF16)
                pcat = jnp.concatenate([p[:BLK], p[BLK:]], axis=1)
                rhs = jnp.concatenate(
                    [jnp.concatenate([vals * m_lo[nk], m_lo[nk]], axis=1),
                     jnp.concatenate([vals * m_hi[nk], m_hi[nk]], axis=1)], axis=0)
                res = _dot(pcat, rhs)
                den = res[:, LANES:]
                og_scr[g, rows(start), :] = res[:, :LANES] / den
                lg_scr[g, rows(start), :] = jnp.where(lo_q, mx[:BLK], mx[BLK:]) + jnp.log(den)

    for r0 in range(0, seq, MERGE_ROWS):
        sl = slice(r0, r0 + MERGE_ROWS)
        l0, l1, l2 = lg_scr[0, sl, :], lg_scr[1, sl, :], lg_scr[2, sl, :]
        mx = jnp.maximum(jnp.maximum(l0, l1), l2)
        e0, e1, e2 = jnp.exp(l0 - mx), jnp.exp(l1 - mx), jnp.exp(l2 - mx)
        inv = 1.0 / (e0 + e1 + e2)
        o = (e0 * inv) * og_scr[0, sl, :] + (e1 * inv) * og_scr[1, sl, :] + (e2 * inv) * og_scr[2, sl, :]
        o_ref[sl, :] = o.astype(o_ref.dtype)


def _attn_prompt(qkv, tabs, batch, seq):
    m = batch * seq

    def spec(g, c):
        return pl.BlockSpec((seq, LANES), lambda b, p: (b, (g * 3 + c) * N_PAIRS + p))

    return pl.pallas_call(
        _attn_prompt_kernel,
        grid=(batch, N_PAIRS),
        in_specs=[spec(g, c) for g in range(N_GROUPS) for c in range(3)]
        + [pl.BlockSpec((N_GROUPS, None, 2 * BLK, 2 * BLK), lambda b, p: (0, p, 0, 0))],
        out_specs=pl.BlockSpec((seq, LANES), lambda b, p: (b, p)),
        out_shape=jax.ShapeDtypeStruct((m, D_MODEL), BF16),
        scratch_shapes=[pltpu.VMEM((6, seq, LANES), F32),
                        pltpu.VMEM((N_GROUPS, seq, LANES), F32),
                        pltpu.VMEM((N_GROUPS, seq, LANES), F32)],
        compiler_params=_cparams(2, 48),
        name="attn_prompt",
    )(*([qkv] * 9), tabs)


ATT_HB = 4
ATT_PAIRS = ATT_HB // 2


def _attn_sample_kernel(q0_ref, q1_ref, q2_ref, c0_ref, c1_ref, c2_ref, t0_ref, t1_ref, t2_ref,
                        tn_ref, o_ref):
    lane = lax.broadcasted_iota(jnp.int32, (HEAD_DIM, LANES), 1)
    pad = jnp.zeros((LANES - 3 * ATT_PAIRS, LANES), F32)
    qts = [jnp.concatenate([q_ref[...], pad], axis=0).T for q_ref in (q0_ref, q1_ref, q2_ref)]
    halves = [jnp.zeros((HEAD_DIM, LANES), F32) for _ in range(2)]
    for hl in range(ATT_HB):
        rows = slice((hl % 2) * HEAD_DIM, (hl % 2 + 1) * HEAD_DIM)
        uns, lses, dens = [], [], []
        for g, (c_ref, t_ref) in enumerate(((c0_ref, t0_ref), (c1_ref, t1_ref), (c2_ref, t2_ref))):
            def col(c, qt=qts[g]):
                j = c * ATT_PAIRS + hl // 2
                return jnp.broadcast_to(qt[rows, j:j + 1], (HEAD_DIM, LANES))
            qb, knb, vnb = col(0), col(1), col(2)
            s_n = jnp.sum(qb * knb, axis=0, keepdims=True) + tn_ref[g, hl:hl + 1, :]
            chunks = [slice(k, k + LANES) for k in range(0, c_ref.shape[-1], LANES)]
            s = [jnp.sum(c_ref[0, hl, :, ck] * qb, axis=0, keepdims=True) + t_ref[hl:hl + 1, ck]
                 for ck in chunks]
            mx = functools.reduce(jnp.maximum, s)
            mx = jnp.maximum(jnp.max(mx, axis=-1, keepdims=True), s_n)
            p = [jnp.exp(sc - mx) for sc in s]
            p_n = jnp.exp(s_n - mx)
            den = jnp.sum(functools.reduce(jnp.add, p), axis=-1, keepdims=True) + p_n
            acc = functools.reduce(jnp.add, [c_ref[1, hl, :, ck] * pc for ck, pc in zip(chunks, p)])
            uns.append(jnp.sum(acc, axis=-1, keepdims=True) + p_n * vnb)
            lses.append(mx + jnp.log(den))
            dens.append(den)
        lmx = jnp.maximum(jnp.maximum(lses[0], lses[1]), lses[2])
        es = [jnp.exp(l - lmx) for l in lses]
        tot = es[0] + es[1] + es[2]
        o = functools.reduce(jnp.add, [(e / (tot * dn)) * un for e, dn, un in zip(es, dens, uns)])
        halves[hl % 2] = jnp.where(lane == hl // 2, o, halves[hl % 2])
    o_ref[...] = jnp.concatenate(halves, axis=0).T[0:ATT_PAIRS, :]


def _attn_sample(qkv, caches_t, layer, tabs, tab_new):
    db = qkv.shape[0]
    nhb = N_HEADS // ATT_HB
    q = qkv.reshape(db, N_GROUPS, 3, nhb, ATT_PAIRS, LANES).transpose(1, 0, 3, 2, 4, 5)
    q = q.reshape(N_GROUPS, db, nhb, 3 * ATT_PAIRS, LANES)
    qspecs, cspecs, tspecs, tabs_v = [], [], [], []
    for g in range(N_GROUPS):
        w = WINDOWS[g]
        qspecs.append(pl.BlockSpec((None, None, None, 3 * ATT_PAIRS, LANES),
                                   lambda b, hb, g=g: (g, b, hb, 0, 0)))
        cspecs.append(pl.BlockSpec((None, None, 2, ATT_HB, HEAD_DIM, w),
                                   lambda b, hb: (layer, b, 0, hb, 0, 0)))
        tabs_v.append(tabs[g].reshape(nhb, ATT_HB, w))
        tspecs.append(pl.BlockSpec((None, ATT_HB, w), lambda b, hb: (hb, 0, 0)))
    o = pl.pallas_call(
        _attn_sample_kernel,
        grid=(db, nhb),
        in_specs=[*qspecs, *cspecs, *tspecs,
                  pl.BlockSpec((N_GROUPS, None, ATT_HB, LANES), lambda b, hb: (0, hb, 0, 0))],
        out_specs=pl.BlockSpec((None, None, ATT_PAIRS, LANES), lambda b, hb: (b, hb, 0, 0)),
        out_shape=jax.ShapeDtypeStruct((db, nhb, ATT_PAIRS, LANES), F32),
        compiler_params=_cparams(2, 32),
        name="attn_sample",
    )(q, q, q, *caches_t, *tabs_v, tab_new.reshape(N_GROUPS, nhb, ATT_HB, LANES))
    return o.reshape(db, D_MODEL)


def _proj_kernel(o_ref, x_ref, w_ref, out_ref):
    out_ref[...] = x_ref[...] + _dot(o_ref[...].astype(BF16), w_ref[...])


def _proj_residual(o, x, w):
    m = x.shape[0]
    tm = _row_tile(m, 1024)
    row = pl.BlockSpec((tm, D_MODEL), lambda i: (i, 0))
    return pl.pallas_call(
        _proj_kernel,
        grid=(m // tm,),
        in_specs=[row, row, _resident((D_MODEL, D_MODEL))],
        out_specs=row,
        out_shape=jax.ShapeDtypeStruct((m, D_MODEL), F32),
        compiler_params=_cparams(1, 40),
        name="proj_residual",
    )(o, x, w)


def _norm_kernel(x_ref, g_ref, o_ref):
    o_ref[...] = _rms(x_ref[...], g_ref[...])


def _final_norm(x, g):
    m = x.shape[0]
    tm = _row_tile(m, 1024)
    return pl.pallas_call(
        _norm_kernel,
        grid=(m // tm,),
        in_specs=[pl.BlockSpec((tm, D_MODEL), lambda i: (i, 0)), _resident((1, D_MODEL))],
        out_specs=pl.BlockSpec((tm, D_MODEL), lambda i: (i, 0)),
        out_shape=jax.ShapeDtypeStruct((m, D_MODEL), F32),
        compiler_params=_cparams(1, 32),
        name="final_norm",
    )(x, g)


def _t5_bucket(dist):
    max_exact = NUM_BUCKETS // 2
    n = jnp.maximum(dist.astype(F32), 1.0)
    large = max_exact + (jnp.log(n / max_exact) / math.log(MAX_DISTANCE / max_exact)
                         * (NUM_BUCKETS - max_exact)).astype(jnp.int32)
    large = jnp.minimum(large, NUM_BUCKETS - 1)
    return jnp.where(dist < max_exact, dist, large)


def _toeplitz(w):
    n = 2 * BLK
    tiled = jnp.tile(w, (1,) * (w.ndim - 1) + (BLK,))[..., :BLK * (n - 1)]
    return tiled.reshape(w.shape[:-1] + (BLK, n - 1))[..., :BLK]


def _bias_tables(rel_bias):
    neg_run = jnp.full((N_HEADS, BLK - 1), NEG, F32)
    neg_one = jnp.full((N_HEADS, 1), NEG, F32)
    prompt, tab, tab_new = [], [], []
    for g in range(N_GROUPS):
        d = DILATIONS[g]
        dist = jnp.arange(BLK + 1, dtype=jnp.int32) * d
        tbl = rel_bias[:, g * N_HEADS:(g + 1) * N_HEADS].astype(F32)
        bias = tbl[_t5_bucket(dist)].T
        rev = bias[:, ::-1]
        w_cur = jnp.concatenate([bias[:, 0:1], neg_run, neg_one, rev[:, 1:BLK]], axis=1)
        w_prev = jnp.concatenate([rev[:, 0:BLK], neg_one, neg_run], axis=1)
        t = jnp.concatenate([_toeplitz(w_prev), _toeplitz(w_cur)], axis=-1)
        prompt.append(t.reshape(N_PAIRS, 2 * BLK, 2 * BLK))
        on_grid = rev[:, 0:BLK, None]
        off_grid = jnp.full((N_HEADS, BLK, d - 1), NEG, F32)
        tab.append(jnp.concatenate([on_grid, off_grid], axis=2).reshape(N_HEADS, BLK * d))
        tab_new.append(jnp.broadcast_to(bias[:, 0:1], (N_HEADS, LANES)))
    return jnp.stack(prompt), tab, jnp.stack(tab_new)


def kernel(x_prompt, x_sample, state_conv, cache_kv_w128, cache_kv_w512, cache_kv_w2048,
           norm_mix, norm_mlp, norm_final,
           conv_w_pw1, conv_b_pw1, conv_w_dw, conv_b_dw, conv_ln_g, conv_ln_b, conv_w_pw2, conv_b_pw2,
           attn_w_qkv, attn_w_o, rel_bias, mlp_w_in, mlp_w_out):
    batch, seq, _ = x_prompt.shape
    db = x_sample.shape[0]
    assert x_sample.shape[1] == 1 and seq % (DILATIONS[-1] * BLK) == 0
    caches = (cache_kv_w128, cache_kv_w512, cache_kv_w2048)
    assert all(c.shape[2] == w for c, w in zip(caches, WINDOWS))

    xp = x_prompt.reshape(batch * seq, D_MODEL)
    xs = x_sample.reshape(db, D_MODEL)
    row = lambda v: v.reshape(1, -1)

    tabs_p, tabs_s, tab_new = _bias_tables(rel_bias)
    caches_t = [jnp.transpose(c, (0, 1, 3, 4, 5, 2)) for c in caches]
    state_t = jnp.transpose(state_conv, (0, 2, 1, 3))

    conv_p, conv_s = [], []
    kv_p = [[] for _ in range(N_GROUPS)]
    kv_s = [[] for _ in range(N_GROUPS)]
    for i in range(DEPTH):
        j = i // 2
        g_mix = row(norm_mix[i])
        if i % 2 == 0:
            w1 = conv_w_pw1[j].astype(BF16)
            w2 = conv_w_pw2[j].astype(BF16)
            prm = (conv_w_dw[j], row(conv_b_dw[j]), row(conv_ln_g[j]), row(conv_ln_b[j]),
                   w2, row(conv_b_pw2[j]))
            a_p = _pw1_glu(xp, g_mix, w1, row(conv_b_pw1[j]))
            a_s = _pw1_glu(xs, g_mix, w1, row(conv_b_pw1[j]))
            xp = _conv_prompt(a_p, xp, batch, seq, *prm)
            xs, ns = _conv_sample(a_s, state_t, j, xs, *prm)
            conv_p.append(a_p.reshape(batch, seq, D_MODEL)[:, seq - HIST:])
            conv_s.append(jnp.transpose(ns, (1, 0, 2)))
        else:
            w_qkv = attn_w_qkv[j].astype(BF16)
            w_o = attn_w_o[j].astype(BF16)
            qkv_p, *kvs = _qkv_proj(xp, g_mix, w_qkv, batch=batch, seq=seq)
            (qkv_s,) = _qkv_proj(xs, g_mix, w_qkv)
            for g in range(N_GROUPS):
                kv_p[g].append(kvs[g].reshape(batch, -1, 2, N_HEADS, HEAD_DIM))
                kv_new = qkv_s[:, (g * 3 + 1) * D_MODEL:(g * 3 + 3) * D_MODEL]
                kv_s[g].append(kv_new.reshape(db, 1, 2, N_HEADS, HEAD_DIM))
            xp = _proj_residual(_attn_prompt(qkv_p, tabs_p, batch, seq), xp, w_o)
            xs = _proj_residual(_attn_sample(qkv_s, caches_t, j, tabs_s, tab_new), xs, w_o)
        g_mlp = row(norm_mlp[i])
        w_in = mlp_w_in[i].astype(BF16)
        w_out = mlp_w_out[i].astype(BF16)
        xp = _mlp(xp, g_mlp, w_in, w_out)
        xs = _mlp(xs, g_mlp, w_in, w_out)

    y_prompt = _final_norm(xp, row(norm_final)).reshape(batch, seq, D_MODEL)
    y_sample = _final_norm(xs, row(norm_final)).reshape(db, 1, D_MODEL)
    return (y_prompt, y_sample, jnp.stack(conv_p), jnp.stack(conv_s),
            jnp.stack(kv_p[0]), jnp.stack(kv_s[0]), jnp.stack(kv_p[1]), jnp.stack(kv_s[1]),
            jnp.stack(kv_p[2]), jnp.stack(kv_s[2]))
```

```python
import functools
import math

import jax
import jax.numpy as jnp
from jax import lax
from jax.experimental import pallas as pl
from jax.experimental.pallas import tpu as pltpu

F32 = jnp.float32
BF16 = jnp.bfloat16

D_MODEL = 1024
DEPTH = 4
HEAD_DIM = 64
N_HEADS = D_MODEL // HEAD_DIM
WINDOWS = (128, 512, 2048)
DILATIONS = (1, 4, 16)
N_GROUPS = 3
BLK = 128
SCALE = HEAD_DIM ** -0.5
NUM_BUCKETS = 32
MAX_DISTANCE = 2048
CONV_WIDTH = 31
HIST = CONV_WIDTH - 1
D_FF = 4 * D_MODEL
EPS = 1e-6
NEG = -1e30

LANES = 128
SUBLANES = 8
N_PAIRS = D_MODEL // LANES
HALO = 32
MIB = 1 << 20

assert all(w // d == BLK for w, d in zip(WINDOWS, DILATIONS))
assert 2 * HEAD_DIM == LANES


def _resident(shape):
    nd = len(shape)
    return pl.BlockSpec(shape, lambda *_: (0,) * nd, pipeline_mode=pl.Buffered(1))


def _cparams(n_axes, vmem_mib):
    return pltpu.CompilerParams(dimension_semantics=("arbitrary",) * n_axes,
                                vmem_limit_bytes=vmem_mib * MIB)


def _dot(a, b):
    return jnp.dot(a, b, preferred_element_type=F32)


def _dot_nt(a, b):
    return lax.dot_general(a, b, (((1,), (1,)), ((), ())), preferred_element_type=F32)


def _rms(x, g):
    return x * lax.rsqrt(jnp.mean(x * x, axis=-1, keepdims=True) + EPS) * g


def _sigmoid(x):
    return 1.0 / (1.0 + jnp.exp(-x))


def _row_tile(m, tile=512):
    return tile if m % tile == 0 else m


def _pw1_kernel(x_ref, g_ref, w_ref, b_ref, a_ref):
    h = _rms(x_ref[...], g_ref[...]).astype(BF16)
    u = _dot(h, w_ref[...]) + b_ref[...]
    a_ref[...] = u[:, :D_MODEL] * _sigmoid(u[:, D_MODEL:])


def _pw1_glu(x, g, w, b):
    m = x.shape[0]
    tm = _row_tile(m)
    return pl.pallas_call(
        _pw1_kernel,
        grid=(m // tm,),
        in_specs=[pl.BlockSpec((tm, D_MODEL), lambda i: (i, 0)),
                  _resident((1, D_MODEL)),
                  _resident((D_MODEL, 2 * D_MODEL)),
                  _resident((1, 2 * D_MODEL))],
        out_specs=pl.BlockSpec((tm, D_MODEL), lambda i: (i, 0)),
        out_shape=jax.ShapeDtypeStruct((m, D_MODEL), F32),
        compiler_params=_cparams(1, 40),
        name="pw1_glu",
    )(x, g, w, b)


CONV_TS = 256
CONV_RC = 64
CONV_CC = 256


def _ln_swish_pw2(c, lng, lnb, w2, b2):
    mu = jnp.mean(c, axis=-1, keepdims=True)
    cc = c - mu
    var = jnp.mean(cc * cc, axis=-1, keepdims=True)
    z = cc * lax.rsqrt(var + EPS) * lng + lnb
    z = z * _sigmoid(z)
    return _dot(z.astype(BF16), w2) + b2


def _conv_prompt_kernel(a_ref, halo_ref, x_ref, wdw_ref, bdw_ref, lng_ref, lnb_ref,
                        w2_ref, b2_ref, o_ref, full_ref, c_ref):
    i = pl.program_id(1)
    full_ref[0:HALO, :] = jnp.where(i > 0, halo_ref[...], 0.0)
    full_ref[HALO:, :] = a_ref[...]
    off = HALO - HIST
    for r0 in range(0, CONV_TS, CONV_RC):
        for c0 in range(0, D_MODEL, CONV_CC):
            cols = slice(c0, c0 + CONV_CC)
            acc = jnp.broadcast_to(bdw_ref[:, cols], (CONV_RC, CONV_CC))
            for phase in range(SUBLANES):
                part = None
                n_rows = CONV_RC + (SUBLANES if phase else 0)
                for k in range(CONV_WIDTH):
                    if (off + k) % SUBLANES != phase:
                        continue
                    base = r0 + off + k - phase
                    term = wdw_ref[k:k + 1, cols] * full_ref[base:base + n_rows, cols]
                    part = term if part is None else part + term
                acc = acc + part[phase:phase + CONV_RC]
            c_ref[r0:r0 + CONV_RC, cols] = acc
    y = _ln_swish_pw2(c_ref[...], lng_ref[...], lnb_ref[...], w2_ref[...], b2_ref[...])
    o_ref[...] = x_ref[...] + y


def _conv_prompt(a, x, batch, seq, wdw, bdw, lng, lnb, w2, b2):
    m = batch * seq
    nt = seq // CONV_TS
    hpt = CONV_TS // HALO
    row = lambda b, i: (b * nt + i, 0)
    return pl.pallas_call(
        _conv_prompt_kernel,
        grid=(batch, nt),
        in_specs=[pl.BlockSpec((CONV_TS, D_MODEL), row),
                  pl.BlockSpec((HALO, D_MODEL),
                               lambda b, i: (jnp.maximum((b * nt + i) * hpt - 1, 0), 0)),
                  pl.BlockSpec((CONV_TS, D_MODEL), row),
                  _resident((CONV_WIDTH, D_MODEL)),
                  _resident((1, D_MODEL)), _resident((1, D_MODEL)), _resident((1, D_MODEL)),
                  _resident((D_MODEL, D_MODEL)), _resident((1, D_MODEL))],
        out_specs=pl.BlockSpec((CONV_TS, D_MODEL), row),
        out_shape=jax.ShapeDtypeStruct((m, D_MODEL), F32),
        scratch_shapes=[pltpu.VMEM((CONV_TS + HALO, D_MODEL), F32),
                        pltpu.VMEM((CONV_TS, D_MODEL), F32)],
        compiler_params=_cparams(2, 32),
        name="conv_prompt",
    )(a, a, x, wdw, bdw, lng, lnb, w2, b2)


CONV_SB = 32


def _conv_sample_kernel(a_ref, st_ref, x_ref, wdw_ref, bdw_ref, lng_ref, lnb_ref,
                        w2_ref, b2_ref, o_ref, ns_ref):
    a = a_ref[...]
    c = a * wdw_ref[HIST:CONV_WIDTH, :] + bdw_ref[...]
    for t in range(HIST):
        st = st_ref[t]
        c = c + st * wdw_ref[t:t + 1, :]
        if t > 0:
            ns_ref[t - 1] = st
    ns_ref[HIST - 1] = a
    y = _ln_swish_pw2(c, lng_ref[...], lnb_ref[...], w2_ref[...], b2_ref[...])
    o_ref[...] = x_ref[...] + y


def _conv_sample(a, state_t, layer, x, wdw, bdw, lng, lnb, w2, b2):
    db = x.shape[0]
    row = lambda i: (i, 0)
    return pl.pallas_call(
        _conv_sample_kernel,
        grid=(db // CONV_SB,),
        in_specs=[pl.BlockSpec((CONV_SB, D_MODEL), row),
                  pl.BlockSpec((None, HIST, CONV_SB, D_MODEL), lambda i: (layer, 0, i, 0)),
                  pl.BlockSpec((CONV_SB, D_MODEL), row),
                  _resident((CONV_WIDTH, D_MODEL)),
                  _resident((1, D_MODEL)), _resident((1, D_MODEL)), _resident((1, D_MODEL)),
                  _resident((D_MODEL, D_MODEL)), _resident((1, D_MODEL))],
        out_specs=[pl.BlockSpec((CONV_SB, D_MODEL), row),
                   pl.BlockSpec((HIST, CONV_SB, D_MODEL), lambda i: (0, i, 0))],
        out_shape=[jax.ShapeDtypeStruct((db, D_MODEL), F32),
                   jax.ShapeDtypeStruct((HIST, db, D_MODEL), F32)],
        compiler_params=_cparams(1, 40),
        name="conv_sample",
    )(a, state_t, x, wdw, bdw, lng, lnb, w2, b2)


MLP_FC = 1024


def _mlp_kernel(x_ref, g_ref, win_ref, wout_ref, o_ref):
    x = x_ref[...]
    h = _rms(x, g_ref[...]).astype(BF16)
    acc = x
    for c0 in range(0, D_FF, MLP_FC):
        u = jnp.maximum(_dot(h, win_ref[:, c0:c0 + MLP_FC]), 0.0)
        acc = acc + _dot((u * u).astype(BF16), wout_ref[c0:c0 + MLP_FC, :])
    o_ref[...] = acc


def _mlp(x, g, w_in, w_out):
    m = x.shape[0]
    tm = _row_tile(m)
    return pl.pallas_call(
        _mlp_kernel,
        grid=(m // tm,),
        in_specs=[pl.BlockSpec((tm, D_MODEL), lambda i: (i, 0)),
                  _resident((1, D_MODEL)),
                  _resident((D_MODEL, D_FF)),
                  _resident((D_FF, D_MODEL))],
        out_specs=pl.BlockSpec((tm, D_MODEL), lambda i: (i, 0)),
        out_shape=jax.ShapeDtypeStruct((m, D_MODEL), F32),
        compiler_params=_cparams(1, 48),
        name="mlp",
    )(x, g, w_in, w_out)


QKV_TM = 256
QKV_COLS = N_GROUPS * 3 * D_MODEL


def _qkv_kernel(keep_rows, n_aliased, x_ref, g_ref, w_ref, *refs):
    qkv_ref, *kv_refs = refs[n_aliased:]
    h = _rms(x_ref[...], g_ref[...]).astype(BF16)
    tm = x_ref.shape[0]
    for g in range(N_GROUPS):
        for c in range(3):
            col = (g * 3 + c) * D_MODEL
            r = _dot(h, w_ref[:, col:col + D_MODEL])
            if c == 0:
                r = r * SCALE
            qkv_ref[:, col:col + D_MODEL] = r.astype(qkv_ref.dtype)
            if kv_refs and c > 0:
                kv_refs[g][:, (c - 1) * D_MODEL:c * D_MODEL] = r[tm - keep_rows[g]:, :]


def _qkv_proj(x, g, w, *, batch=None, seq=None, layer=0, kv_bufs=()):
    m = x.shape[0]
    prompt = batch is not None
    tm = QKV_TM if prompt else m
    in_specs = [pl.BlockSpec((tm, D_MODEL), lambda i: (i, 0)),
                _resident((1, D_MODEL)),
                _resident((D_MODEL, QKV_COLS))]
    out_specs = [pl.BlockSpec((tm, QKV_COLS), lambda i: (i, 0))]
    out_shape = [jax.ShapeDtypeStruct((m, QKV_COLS), BF16 if prompt else F32)]
    keep_rows = ()
    if prompt:
        tpb = seq // tm
        for win in WINDOWS:
            keep = min(win, seq)
            rows = min(keep, tm)
            first = tpb - keep // rows
            keep_rows += (rows,)
            out_specs.append(pl.BlockSpec(
                (None, None, rows, 2 * D_MODEL),
                lambda i, first=first: (layer, i // tpb, jnp.maximum(i % tpb - first, 0), 0)))
            out_shape.append(jax.ShapeDtypeStruct((DEPTH // 2, batch, keep, 2 * D_MODEL), F32))
    in_specs += [pl.BlockSpec(memory_space=pl.ANY)] * len(kv_bufs)
    return pl.pallas_call(
        functools.partial(_qkv_kernel, keep_rows, len(kv_bufs)),
        grid=(m // tm,),
        in_specs=in_specs, out_specs=out_specs, out_shape=out_shape,
        input_output_aliases={3 + k: 1 + k for k in range(len(kv_bufs))},
        compiler_params=_cparams(1, 56),
        name="qkv_proj",
    )(x, g, w, *kv_bufs)


MERGE_ROWS = 256


def _attn_prompt_kernel(*refs):
    qkv_refs, (tab_ref, o_ref, f_scr, og_scr, lg_scr) = refs[:9], refs[9:]
    seq = o_ref.shape[0]
    lo_q = lax.broadcasted_iota(jnp.int32, (BLK, LANES), 1) < HEAD_DIM
    m_lo, m_hi = {}, {}
    for n_rows in (BLK, 2 * BLK):
        lo = lax.broadcasted_iota(jnp.int32, (n_rows, LANES), 1) < HEAD_DIM
        m_lo[n_rows] = jnp.where(lo, 1.0, 0.0).astype(BF16)
        m_hi[n_rows] = jnp.where(lo, 0.0, 1.0).astype(BF16)

    for g in range(N_GROUPS):
        d = DILATIONS[g]
        if d == 1:
            src = qkv_refs[0:3]
        else:
            src = []
            for c in range(3):
                f_scr[(g - 1) * 3 + c] = qkv_refs[g * 3 + c][...].astype(F32)
                src.append(f_scr.at[(g - 1) * 3 + c])

        def rows(start, d=d):
            return pl.ds(start, BLK) if d == 1 else pl.ds(start, BLK, stride=d)

        def load(c, start, src=src):
            return src[c][rows(start), :].astype(BF16)

        n_blk = seq // d // BLK
        for r in range(d):
            for n in range(n_blk):
                start = r + n * BLK * d
                q = load(0, start)
                kc, vc = load(1, start), load(2, start)
                if n == 0:
                    keys, vals, tab = kc, vc, tab_ref[g, :, BLK:]
                else:
                    prev = start - BLK * d
                    keys = jnp.concatenate([load(1, prev), kc], axis=0)
                    vals = jnp.concatenate([load(2, prev), vc], axis=0)
                    tab = tab_ref[g]
                nk = keys.shape[0]
                q2 = jnp.concatenate([q * m_lo[BLK], q * m_hi[BLK]], axis=0)
                s = _dot_nt(q2, keys) + tab
                mx = jnp.max(s, axis=-1, keepdims=True)
                p = jnp.exp(s - mx).astype(BF16)
                pcat = jnp.concatenate([p[:BLK], p[BLK:]], axis=1)
                rhs = jnp.concatenate(
                    [jnp.concatenate([vals * m_lo[nk], m_lo[nk]], axis=1),
                     jnp.concatenate([vals * m_hi[nk], m_hi[nk]], axis=1)], axis=0)
                res = _dot(pcat, rhs)
                den = res[:, LANES:]
                og_scr[g, rows(start), :] = res[:, :LANES] / den
                lg_scr[g, rows(start), :] = jnp.where(lo_q, mx[:BLK], mx[BLK:]) + jnp.log(den)

    for r0 in range(0, seq, MERGE_ROWS):
        sl = slice(r0, r0 + MERGE_ROWS)
        l0, l1, l2 = lg_scr[0, sl, :], lg_scr[1, sl, :], lg_scr[2, sl, :]
        mx = jnp.maximum(jnp.maximum(l0, l1), l2)
        e0, e1, e2 = jnp.exp(l0 - mx), jnp.exp(l1 - mx), jnp.exp(l2 - mx)
        inv = 1.0 / (e0 + e1 + e2)
        o = (e0 * inv) * og_scr[0, sl, :] + (e1 * inv) * og_scr[1, sl, :] + (e2 * inv) * og_scr[2, sl, :]
        o_ref[sl, :] = o.astype(o_ref.dtype)


def _attn_prompt(qkv, tabs, batch, seq):
    m = batch * seq

    def spec(g, c):
        return pl.BlockSpec((seq, LANES), lambda b, p: (b, (g * 3 + c) * N_PAIRS + p))

    return pl.pallas_call(
        _attn_prompt_kernel,
        grid=(batch, N_PAIRS),
        in_specs=[spec(g, c) for g in range(N_GROUPS) for c in range(3)]
        + [pl.BlockSpec((N_GROUPS, None, 2 * BLK, 2 * BLK), lambda b, p: (0, p, 0, 0))],
        out_specs=pl.BlockSpec((seq, LANES), lambda b, p: (b, p)),
        out_shape=jax.ShapeDtypeStruct((m, D_MODEL), BF16),
        scratch_shapes=[pltpu.VMEM((6, seq, LANES), F32),
                        pltpu.VMEM((N_GROUPS, seq, LANES), F32),
                        pltpu.VMEM((N_GROUPS, seq, LANES), F32)],
        compiler_params=_cparams(2, 48),
        name="attn_prompt",
    )(*([qkv] * 9), tabs)


ATT_HB = SUBLANES
ATT_PAIRS = ATT_HB // 2


def _attn_sample_kernel(q0_ref, q1_ref, q2_ref, c0_ref, c1_ref, c2_ref, t0_ref, t1_ref, t2_ref,
                        tn_ref, o_ref):
    lane = lax.broadcasted_iota(jnp.int32, (HEAD_DIM, LANES), 1)
    pad = jnp.zeros((LANES - 3 * ATT_PAIRS, LANES), F32)
    heads = range(ATT_HB)
    stack = lambda per_head: jnp.concatenate(per_head, axis=0)
    groups = ((q0_ref, c0_ref, t0_ref), (q1_ref, c1_ref, t1_ref), (q2_ref, c2_ref, t2_ref))

    cols, s_new, scores, chunks = [], [], [], []
    for g, (q_ref, c_ref, t_ref) in enumerate(groups):
        qt = jnp.concatenate([q_ref[...], pad], axis=0).T
        col = [[jnp.broadcast_to(qt[(hl % 2) * HEAD_DIM:(hl % 2 + 1) * HEAD_DIM,
                                    c * ATT_PAIRS + hl // 2:c * ATT_PAIRS + hl // 2 + 1],
                                 (HEAD_DIM, LANES)) for hl in heads] for c in range(3)]
        cols.append(col)
        cks = [slice(k, k + LANES) for k in range(0, c_ref.shape[-1], LANES)]
        chunks.append(cks)
        s_new.append(stack([jnp.sum(col[0][hl] * col[1][hl], axis=0, keepdims=True) for hl in heads])
                     + tn_ref[g])
        scores.append([stack([jnp.sum(c_ref[0, hl, :, ck] * col[0][hl], axis=0, keepdims=True)
                              for hl in heads]) + t_ref[:, ck] for ck in cks])

    probs, p_new, dens, lses = [], [], [], []
    for g in range(N_GROUPS):
        mx = jnp.max(functools.reduce(jnp.maximum, scores[g]), axis=-1, keepdims=True)
        mx = jnp.maximum(mx, s_new[g])
        p = [jnp.exp(sc - mx) for sc in scores[g]]
        pn = jnp.exp(s_new[g] - mx)
        den = jnp.sum(functools.reduce(jnp.add, p), axis=-1, keepdims=True) + pn
        probs.append(p)
        p_new.append(pn)
        dens.append(den)
        lses.append(mx + jnp.log(den))

    lmx = jnp.maximum(jnp.maximum(lses[0], lses[1]), lses[2])
    es = [jnp.exp(l - lmx) for l in lses]
    tot = es[0] + es[1] + es[2]
    coef = [e / (tot * dn) for e, dn in zip(es, dens)]

    halves = [jnp.zeros((HEAD_DIM, LANES), F32) for _ in range(2)]
    for hl in heads:
        o = None
        for g, (_, c_ref, _) in enumerate(groups):
            acc = functools.reduce(jnp.add, [c_ref[1, hl, :, ck] * p[hl:hl + 1, :]
                                             for ck, p in zip(chunks[g], probs[g])])
            un = jnp.sum(acc, axis=-1, keepdims=True) + p_new[g][hl:hl + 1, :] * cols[g][2][hl]
            o = coef[g][hl:hl + 1, :] * un if o is None else o + coef[g][hl:hl + 1, :] * un
        halves[hl % 2] = jnp.where(lane == hl // 2, o, halves[hl % 2])
    o_ref[...] = jnp.concatenate(halves, axis=0).T[0:ATT_PAIRS, :]


def _attn_sample(qkv, caches_t, layer, tabs, tab_new):
    db = qkv.shape[0]
    nhb = N_HEADS // ATT_HB
    q = qkv.reshape(db, N_GROUPS, 3, nhb, ATT_PAIRS, LANES).transpose(1, 0, 3, 2, 4, 5)
    q = q.reshape(N_GROUPS, db, nhb, 3 * ATT_PAIRS, LANES)
    qspecs, cspecs, tspecs, tabs_v = [], [], [], []
    for g in range(N_GROUPS):
        w = WINDOWS[g]
        qspecs.append(pl.BlockSpec((None, None, None, 3 * ATT_PAIRS, LANES),
                                   lambda b, hb, g=g: (g, b, hb, 0, 0)))
        cspecs.append(pl.BlockSpec((None, None, 2, ATT_HB, HEAD_DIM, w),
                                   lambda b, hb: (layer, b, 0, hb, 0, 0)))
        tabs_v.append(tabs[g].reshape(nhb, ATT_HB, w))
        tspecs.append(pl.BlockSpec((None, ATT_HB, w), lambda b, hb: (hb, 0, 0)))
    o = pl.pallas_call(
        _attn_sample_kernel,
        grid=(db, nhb),
        in_specs=[*qspecs, *cspecs, *tspecs,
                  pl.BlockSpec((N_GROUPS, None, ATT_HB, LANES), lambda b, hb: (0, hb, 0, 0))],
        out_specs=pl.BlockSpec((None, None, ATT_PAIRS, LANES), lambda b, hb: (b, hb, 0, 0)),
        out_shape=jax.ShapeDtypeStruct((db, nhb, ATT_PAIRS, LANES), F32),
        compiler_params=_cparams(2, 40),
        name="attn_sample",
    )(q, q, q, *caches_t, *tabs_v, tab_new.reshape(N_GROUPS, nhb, ATT_HB, LANES))
    return o.reshape(db, D_MODEL)


def _proj_kernel(o_ref, x_ref, w_ref, out_ref):
    out_ref[...] = x_ref[...] + _dot(o_ref[...].astype(BF16), w_ref[...])


def _proj_residual(o, x, w):
    m = x.shape[0]
    tm = _row_tile(m, 1024)
    row = pl.BlockSpec((tm, D_MODEL), lambda i: (i, 0))
    return pl.pallas_call(
        _proj_kernel,
        grid=(m // tm,),
        in_specs=[row, row, _resident((D_MODEL, D_MODEL))],
        out_specs=row,
        out_shape=jax.ShapeDtypeStruct((m, D_MODEL), F32),
        compiler_params=_cparams(1, 40),
        name="proj_residual",
    )(o, x, w)


def _norm_kernel(x_ref, g_ref, o_ref):
    o_ref[...] = _rms(x_ref[...], g_ref[...])


def _final_norm(x, g):
    m = x.shape[0]
    tm = _row_tile(m, 1024)
    return pl.pallas_call(
        _norm_kernel,
        grid=(m // tm,),
        in_specs=[pl.BlockSpec((tm, D_MODEL), lambda i: (i, 0)), _resident((1, D_MODEL))],
        out_specs=pl.BlockSpec((tm, D_MODEL), lambda i: (i, 0)),
        out_shape=jax.ShapeDtypeStruct((m, D_MODEL), F32),
        compiler_params=_cparams(1, 32),
        name="final_norm",
    )(x, g)


def _t5_bucket(dist):
    max_exact = NUM_BUCKETS // 2
    n = jnp.maximum(dist.astype(F32), 1.0)
    large = max_exact + (jnp.log(n / max_exact) / math.log(MAX_DISTANCE / max_exact)
                         * (NUM_BUCKETS - max_exact)).astype(jnp.int32)
    large = jnp.minimum(large, NUM_BUCKETS - 1)
    return jnp.where(dist < max_exact, dist, large)


def _toeplitz(w):
    n = 2 * BLK
    tiled = jnp.tile(w, (1,) * (w.ndim - 1) + (BLK,))[..., :BLK * (n - 1)]
    return tiled.reshape(w.shape[:-1] + (BLK, n - 1))[..., :BLK]


def _bias_tables(rel_bias):
    neg_run = jnp.full((N_HEADS, BLK - 1), NEG, F32)
    neg_one = jnp.full((N_HEADS, 1), NEG, F32)
    prompt, tab, tab_new = [], [], []
    for g in range(N_GROUPS):
        d = DILATIONS[g]
        dist = jnp.arange(BLK + 1, dtype=jnp.int32) * d
        tbl = rel_bias[:, g * N_HEADS:(g + 1) * N_HEADS].astype(F32)
        bias = tbl[_t5_bucket(dist)].T
        rev = bias[:, ::-1]
        w_cur = jnp.concatenate([bias[:, 0:1], neg_run, neg_one, rev[:, 1:BLK]], axis=1)
        w_prev = jnp.concatenate([rev[:, 0:BLK], neg_one, neg_run], axis=1)
        t = jnp.concatenate([_toeplitz(w_prev), _toeplitz(w_cur)], axis=-1)
        prompt.append(t.reshape(N_PAIRS, 2 * BLK, 2 * BLK))
        on_grid = rev[:, 0:BLK, None]
        off_grid = jnp.full((N_HEADS, BLK, d - 1), NEG, F32)
        tab.append(jnp.concatenate([on_grid, off_grid], axis=2).reshape(N_HEADS, BLK * d))
        tab_new.append(jnp.broadcast_to(bias[:, 0:1], (N_HEADS, LANES)))
    return jnp.stack(prompt), tab, jnp.stack(tab_new)


def kernel(x_prompt, x_sample, state_conv, cache_kv_w128, cache_kv_w512, cache_kv_w2048,
           norm_mix, norm_mlp, norm_final,
           conv_w_pw1, conv_b_pw1, conv_w_dw, conv_b_dw, conv_ln_g, conv_ln_b, conv_w_pw2, conv_b_pw2,
           attn_w_qkv, attn_w_o, rel_bias, mlp_w_in, mlp_w_out):
    batch, seq, _ = x_prompt.shape
    db = x_sample.shape[0]
    assert x_sample.shape[1] == 1 and seq % (DILATIONS[-1] * BLK) == 0
    caches = (cache_kv_w128, cache_kv_w512, cache_kv_w2048)
    assert all(c.shape[2] == w for c, w in zip(caches, WINDOWS))

    xp = x_prompt.reshape(batch * seq, D_MODEL)
    xs = x_sample.reshape(db, D_MODEL)
    row = lambda v: v.reshape(1, -1)

    tabs_p, tabs_s, tab_new = _bias_tables(rel_bias)
    caches_t = [jnp.transpose(c, (0, 1, 3, 4, 5, 2)) for c in caches]
    state_t = jnp.transpose(state_conv, (0, 2, 1, 3))

    conv_p, conv_s = [], []
    kv_bufs = ()
    kv_s = [[] for _ in range(N_GROUPS)]
    for i in range(DEPTH):
        j = i // 2
        g_mix = row(norm_mix[i])
        if i % 2 == 0:
            w1 = conv_w_pw1[j].astype(BF16)
            w2 = conv_w_pw2[j].astype(BF16)
            prm = (conv_w_dw[j], row(conv_b_dw[j]), row(conv_ln_g[j]), row(conv_ln_b[j]),
                   w2, row(conv_b_pw2[j]))
            a_p = _pw1_glu(xp, g_mix, w1, row(conv_b_pw1[j]))
            a_s = _pw1_glu(xs, g_mix, w1, row(conv_b_pw1[j]))
            xp = _conv_prompt(a_p, xp, batch, seq, *prm)
            xs, ns = _conv_sample(a_s, state_t, j, xs, *prm)
            conv_p.append(a_p.reshape(batch, seq, D_MODEL)[:, seq - HIST:])
            conv_s.append(jnp.transpose(ns, (1, 0, 2)))
        else:
            w_qkv = attn_w_qkv[j].astype(BF16)
            w_o = attn_w_o[j].astype(BF16)
            qkv_p, *kv_bufs = _qkv_proj(xp, g_mix, w_qkv, batch=batch, seq=seq, layer=j, kv_bufs=kv_bufs)
            (qkv_s,) = _qkv_proj(xs, g_mix, w_qkv)
            for g in range(N_GROUPS):
                kv_new = qkv_s[:, (g * 3 + 1) * D_MODEL:(g * 3 + 3) * D_MODEL]
                kv_s[g].append(kv_new.reshape(db, 1, 2, N_HEADS, HEAD_DIM))
            xp = _proj_residual(_attn_prompt(qkv_p, tabs_p, batch, seq), xp, w_o)
            xs = _proj_residual(_attn_sample(qkv_s, caches_t, j, tabs_s, tab_new), xs, w_o)
        g_mlp = row(norm_mlp[i])
        w_in = mlp_w_in[i].astype(BF16)
        w_out = mlp_w_out[i].astype(BF16)
        xp = _mlp(xp, g_mlp, w_in, w_out)
        xs = _mlp(xs, g_mlp, w_in, w_out)

    y_prompt = _final_norm(xp, row(norm_final)).reshape(batch, seq, D_MODEL)
    y_sample = _final_norm(xs, row(norm_final)).reshape(db, 1, D_MODEL)
    kv_p = [b.reshape(b.shape[:3] + (2, N_HEADS, HEAD_DIM)) for b in kv_bufs]
    return (y_prompt, y_sample, jnp.stack(conv_p), jnp.stack(conv_s),
            kv_p[0], jnp.stack(kv_s[0]), kv_p[1], jnp.stack(kv_s[1]), kv_p[2], jnp.stack(kv_s[2]))
```

```python
import functools
import math

import jax
import jax.numpy as jnp
from jax import lax
from jax.experimental import pallas as pl
from jax.experimental.pallas import tpu as pltpu

F32 = jnp.float32
BF16 = jnp.bfloat16

D_MODEL = 1024
DEPTH = 4
HEAD_DIM = 64
N_HEADS = D_MODEL // HEAD_DIM
WINDOWS = (128, 512, 2048)
DILATIONS = (1, 4, 16)
N_GROUPS = 3
BLK = 128
SCALE = HEAD_DIM ** -0.5
NUM_BUCKETS = 32
MAX_DISTANCE = 2048
CONV_WIDTH = 31
HIST = CONV_WIDTH - 1
D_FF = 4 * D_MODEL
EPS = 1e-6
NEG = -1e30

LANES = 128
SUBLANES = 8
N_PAIRS = D_MODEL // LANES
HALO = 32
MIB = 1 << 20

assert all(w // d == BLK for w, d in zip(WINDOWS, DILATIONS))
assert 2 * HEAD_DIM == LANES


def _resident(shape):
    nd = len(shape)
    return pl.BlockSpec(shape, lambda *_: (0,) * nd, pipeline_mode=pl.Buffered(1))


def _layer_slab(shape, layer):
    nd = len(shape)
    return pl.BlockSpec((None,) + tuple(shape), lambda *_: (layer,) + (0,) * nd, pipeline_mode=pl.Buffered(1))


def _cparams(n_axes, vmem_mib):
    return pltpu.CompilerParams(dimension_semantics=("arbitrary",) * n_axes,
                                vmem_limit_bytes=vmem_mib * MIB)


def _dot(a, b):
    return jnp.dot(a, b, preferred_element_type=F32)


def _dot_nt(a, b):
    return lax.dot_general(a, b, (((1,), (1,)), ((), ())), preferred_element_type=F32)


def _rms(x, g):
    return x * lax.rsqrt(jnp.mean(x * x, axis=-1, keepdims=True) + EPS) * g


def _sigmoid(x):
    return 1.0 / (1.0 + jnp.exp(-x))


def _row_tile(m, tile=512):
    return tile if m % tile == 0 else m


def _pw1_kernel(x_ref, g_ref, w_ref, b_ref, a_ref):
    h = _rms(x_ref[...], g_ref[...]).astype(BF16)
    u = _dot(h, w_ref[...]) + b_ref[...]
    a_ref[...] = u[:, :D_MODEL] * _sigmoid(u[:, D_MODEL:])


def _pw1_glu(x, g, w, b, layer):
    m = x.shape[0]
    tm = _row_tile(m)
    return pl.pallas_call(
        _pw1_kernel,
        grid=(m // tm,),
        in_specs=[pl.BlockSpec((tm, D_MODEL), lambda i: (i, 0)),
                  _resident((1, D_MODEL)),
                  _layer_slab((D_MODEL, 2 * D_MODEL), layer),
                  _resident((1, 2 * D_MODEL))],
        out_specs=pl.BlockSpec((tm, D_MODEL), lambda i: (i, 0)),
        out_shape=jax.ShapeDtypeStruct((m, D_MODEL), F32),
        compiler_params=_cparams(1, 40),
        name="pw1_glu",
    )(x, g, w, b)


CONV_TS = 256
CONV_RC = 64
CONV_CC = 256


def _ln_swish_pw2(c, lng, lnb, w2, b2):
    mu = jnp.mean(c, axis=-1, keepdims=True)
    cc = c - mu
    var = jnp.mean(cc * cc, axis=-1, keepdims=True)
    z = cc * lax.rsqrt(var + EPS) * lng + lnb
    z = z * _sigmoid(z)
    return _dot(z.astype(BF16), w2) + b2


def _conv_prompt_kernel(a_ref, halo_ref, x_ref, wdw_ref, bdw_ref, lng_ref, lnb_ref,
                        w2_ref, b2_ref, o_ref, full_ref, c_ref):
    i = pl.program_id(1)
    full_ref[0:HALO, :] = jnp.where(i > 0, halo_ref[...], 0.0)
    full_ref[HALO:, :] = a_ref[...]
    off = HALO - HIST
    for r0 in range(0, CONV_TS, CONV_RC):
        for c0 in range(0, D_MODEL, CONV_CC):
            cols = slice(c0, c0 + CONV_CC)
            acc = jnp.broadcast_to(bdw_ref[:, cols], (CONV_RC, CONV_CC))
            for phase in range(SUBLANES):
                part = None
                n_rows = CONV_RC + (SUBLANES if phase else 0)
                for k in range(CONV_WIDTH):
                    if (off + k) % SUBLANES != phase:
                        continue
                    base = r0 + off + k - phase
                    term = wdw_ref[k:k + 1, cols] * full_ref[base:base + n_rows, cols]
                    part = term if part is None else part + term
                acc = acc + part[phase:phase + CONV_RC]
            c_ref[r0:r0 + CONV_RC, cols] = acc
    y = _ln_swish_pw2(c_ref[...], lng_ref[...], lnb_ref[...], w2_ref[...], b2_ref[...])
    o_ref[...] = x_ref[...] + y


def _conv_prompt(a, x, batch, seq, layer, wdw, bdw, lng, lnb, w2, b2):
    m = batch * seq
    nt = seq // CONV_TS
    hpt = CONV_TS // HALO
    row = lambda b, i: (b * nt + i, 0)
    return pl.pallas_call(
        _conv_prompt_kernel,
        grid=(batch, nt),
        in_specs=[pl.BlockSpec((CONV_TS, D_MODEL), row),
                  pl.BlockSpec((HALO, D_MODEL),
                               lambda b, i: (jnp.maximum((b * nt + i) * hpt - 1, 0), 0)),
                  pl.BlockSpec((CONV_TS, D_MODEL), row),
                  _resident((CONV_WIDTH, D_MODEL)),
                  _resident((1, D_MODEL)), _resident((1, D_MODEL)), _resident((1, D_MODEL)),
                  _layer_slab((D_MODEL, D_MODEL), layer), _resident((1, D_MODEL))],
        out_specs=pl.BlockSpec((CONV_TS, D_MODEL), row),
        out_shape=jax.ShapeDtypeStruct((m, D_MODEL), F32),
        scratch_shapes=[pltpu.VMEM((CONV_TS + HALO, D_MODEL), F32),
                        pltpu.VMEM((CONV_TS, D_MODEL), F32)],
        compiler_params=_cparams(2, 32),
        name="conv_prompt",
    )(a, a, x, wdw, bdw, lng, lnb, w2, b2)


CONV_SB = 32


def _conv_sample_kernel(a_ref, st_ref, x_ref, wdw_ref, bdw_ref, lng_ref, lnb_ref,
                        w2_ref, b2_ref, o_ref, ns_ref):
    a = a_ref[...]
    c = a * wdw_ref[HIST:CONV_WIDTH, :] + bdw_ref[...]
    for t in range(HIST):
        st = st_ref[t]
        c = c + st * wdw_ref[t:t + 1, :]
        if t > 0:
            ns_ref[t - 1] = st
    ns_ref[HIST - 1] = a
    y = _ln_swish_pw2(c, lng_ref[...], lnb_ref[...], w2_ref[...], b2_ref[...])
    o_ref[...] = x_ref[...] + y


def _conv_sample(a, state_t, x, layer, wdw, bdw, lng, lnb, w2, b2):
    db = x.shape[0]
    row = lambda i: (i, 0)
    return pl.pallas_call(
        _conv_sample_kernel,
        grid=(db // CONV_SB,),
        in_specs=[pl.BlockSpec((CONV_SB, D_MODEL), row),
                  pl.BlockSpec((None, HIST, CONV_SB, D_MODEL), lambda i: (layer, 0, i, 0)),
                  pl.BlockSpec((CONV_SB, D_MODEL), row),
                  _resident((CONV_WIDTH, D_MODEL)),
                  _resident((1, D_MODEL)), _resident((1, D_MODEL)), _resident((1, D_MODEL)),
                  _layer_slab((D_MODEL, D_MODEL), layer), _resident((1, D_MODEL))],
        out_specs=[pl.BlockSpec((CONV_SB, D_MODEL), row),
                   pl.BlockSpec((HIST, CONV_SB, D_MODEL), lambda i: (0, i, 0))],
        out_shape=[jax.ShapeDtypeStruct((db, D_MODEL), F32),
                   jax.ShapeDtypeStruct((HIST, db, D_MODEL), F32)],
        compiler_params=_cparams(1, 40),
        name="conv_sample",
    )(a, state_t, x, wdw, bdw, lng, lnb, w2, b2)


MLP_FC = 1024


def _mlp_kernel(x_ref, g_ref, win_ref, wout_ref, o_ref):
    x = x_ref[...]
    h = _rms(x, g_ref[...]).astype(BF16)
    acc = x
    for c0 in range(0, D_FF, MLP_FC):
        u = jnp.maximum(_dot(h, win_ref[:, c0:c0 + MLP_FC]), 0.0)
        acc = acc + _dot((u * u).astype(BF16), wout_ref[c0:c0 + MLP_FC, :])
    o_ref[...] = acc


def _mlp(x, g, w_in, w_out, layer):
    m = x.shape[0]
    tm = _row_tile(m)
    return pl.pallas_call(
        _mlp_kernel,
        grid=(m // tm,),
        in_specs=[pl.BlockSpec((tm, D_MODEL), lambda i: (i, 0)),
                  _resident((1, D_MODEL)),
                  _layer_slab((D_MODEL, D_FF), layer),
                  _layer_slab((D_FF, D_MODEL), layer)],
        out_specs=pl.BlockSpec((tm, D_MODEL), lambda i: (i, 0)),
        out_shape=jax.ShapeDtypeStruct((m, D_MODEL), F32),
        compiler_params=_cparams(1, 48),
        name="mlp",
    )(x, g, w_in, w_out)


QKV_TM = 256
QKV_COLS = N_GROUPS * 3 * D_MODEL


def _qkv_kernel(keep_rows, n_aliased, x_ref, g_ref, w_ref, *refs):
    qkv_ref, *kv_refs = refs[n_aliased:]
    h = _rms(x_ref[...], g_ref[...]).astype(BF16)
    tm = x_ref.shape[0]
    for g in range(N_GROUPS):
        for c in range(3):
            col = (g * 3 + c) * D_MODEL
            r = _dot(h, w_ref[:, col:col + D_MODEL])
            if c == 0:
                r = r * SCALE
            qkv_ref[:, col:col + D_MODEL] = r.astype(qkv_ref.dtype)
            if kv_refs and c > 0:
                kv_refs[g][:, (c - 1) * D_MODEL:c * D_MODEL] = r[tm - keep_rows[g]:, :]


def _qkv_proj(x, g, w, layer, *, batch=None, seq=None, kv_bufs=()):
    m = x.shape[0]
    prompt = batch is not None
    tm = QKV_TM if prompt else m
    in_specs = [pl.BlockSpec((tm, D_MODEL), lambda i: (i, 0)),
                _resident((1, D_MODEL)),
                _layer_slab((D_MODEL, QKV_COLS), layer)]
    out_specs = [pl.BlockSpec((tm, QKV_COLS), lambda i: (i, 0))]
    out_shape = [jax.ShapeDtypeStruct((m, QKV_COLS), BF16 if prompt else F32)]
    keep_rows = ()
    if prompt:
        tpb = seq // tm
        for win in WINDOWS:
            keep = min(win, seq)
            rows = min(keep, tm)
            first = tpb - keep // rows
            keep_rows += (rows,)
            out_specs.append(pl.BlockSpec(
                (None, None, rows, 2 * D_MODEL),
                lambda i, first=first: (layer, i // tpb, jnp.maximum(i % tpb - first, 0), 0)))
            out_shape.append(jax.ShapeDtypeStruct((DEPTH // 2, batch, keep, 2 * D_MODEL), F32))
    in_specs += [pl.BlockSpec(memory_space=pl.ANY)] * len(kv_bufs)
    return pl.pallas_call(
        functools.partial(_qkv_kernel, keep_rows, len(kv_bufs)),
        grid=(m // tm,),
        in_specs=in_specs, out_specs=out_specs, out_shape=out_shape,
        input_output_aliases={3 + k: 1 + k for k in range(len(kv_bufs))},
        compiler_params=_cparams(1, 56),
        name="qkv_proj",
    )(x, g, w, *kv_bufs)


MERGE_ROWS = 256


def _attn_prompt_kernel(*refs):
    qkv_refs, (tab_ref, o_ref, f_scr, og_scr, lg_scr) = refs[:9], refs[9:]
    seq = o_ref.shape[0]
    lo_q = lax.broadcasted_iota(jnp.int32, (BLK, LANES), 1) < HEAD_DIM
    m_lo, m_hi = {}, {}
    for n_rows in (BLK, 2 * BLK):
        lo = lax.broadcasted_iota(jnp.int32, (n_rows, LANES), 1) < HEAD_DIM
        m_lo[n_rows] = jnp.where(lo, 1.0, 0.0).astype(BF16)
        m_hi[n_rows] = jnp.where(lo, 0.0, 1.0).astype(BF16)

    for g in range(N_GROUPS):
        d = DILATIONS[g]
        if d == 1:
            src = qkv_refs[0:3]
        else:
            src = []
            for c in range(3):
                f_scr[(g - 1) * 3 + c] = qkv_refs[g * 3 + c][...].astype(F32)
                src.append(f_scr.at[(g - 1) * 3 + c])

        def rows(start, d=d):
            return pl.ds(start, BLK) if d == 1 else pl.ds(start, BLK, stride=d)

        def load(c, start, src=src):
            return src[c][rows(start), :].astype(BF16)

        n_blk = seq // d // BLK
        for r in range(d):
            for n in range(n_blk):
                start = r + n * BLK * d
                q = load(0, start)
                kc, vc = load(1, start), load(2, start)
                if n == 0:
                    keys, vals, tab = kc, vc, tab_ref[g, :, BLK:]
                else:
                    prev = start - BLK * d
                    keys = jnp.concatenate([load(1, prev), kc], axis=0)
                    vals = jnp.concatenate([load(2, prev), vc], axis=0)
                    tab = tab_ref[g]
                nk = keys.shape[0]
                q2 = jnp.concatenate([q * m_lo[BLK], q * m_hi[BLK]], axis=0)
                s = _dot_nt(q2, keys) + tab
                mx = jnp.max(s, axis=-1, keepdims=True)
                p = jnp.exp(s - mx).astype(BF16)
                pcat = jnp.concatenate([p[:BLK], p[BLK:]], axis=1)
                rhs = jnp.concatenate(
                    [jnp.concatenate([vals * m_lo[nk], m_lo[nk]], axis=1),
                     jnp.concatenate([vals * m_hi[nk], m_hi[nk]], axis=1)], axis=0)
                res = _dot(pcat, rhs)
                den = res[:, LANES:]
                og_scr[g, rows(start), :] = res[:, :LANES] / den
                lg_scr[g, rows(start), :] = jnp.where(lo_q, mx[:BLK], mx[BLK:]) + jnp.log(den)

    for r0 in range(0, seq, MERGE_ROWS):
        sl = slice(r0, r0 + MERGE_ROWS)
        l0, l1, l2 = lg_scr[0, sl, :], lg_scr[1, sl, :], lg_scr[2, sl, :]
        mx = jnp.maximum(jnp.maximum(l0, l1), l2)
        e0, e1, e2 = jnp.exp(l0 - mx), jnp.exp(l1 - mx), jnp.exp(l2 - mx)
        inv = 1.0 / (e0 + e1 + e2)
        o = (e0 * inv) * og_scr[0, sl, :] + (e1 * inv) * og_scr[1, sl, :] + (e2 * inv) * og_scr[2, sl, :]
        o_ref[sl, :] = o.astype(o_ref.dtype)


def _attn_prompt(qkv, tabs, batch, seq):
    m = batch * seq

    def spec(g, c):
        return pl.BlockSpec((seq, LANES), lambda b, p: (b, (g * 3 + c) * N_PAIRS + p))

    return pl.pallas_call(
        _attn_prompt_kernel,
        grid=(batch, N_PAIRS),
        in_specs=[spec(g, c) for g in range(N_GROUPS) for c in range(3)]
        + [pl.BlockSpec((N_GROUPS, None, 2 * BLK, 2 * BLK), lambda b, p: (0, p, 0, 0))],
        out_specs=pl.BlockSpec((seq, LANES), lambda b, p: (b, p)),
        out_shape=jax.ShapeDtypeStruct((m, D_MODEL), BF16),
        scratch_shapes=[pltpu.VMEM((6, seq, LANES), F32),
                        pltpu.VMEM((N_GROUPS, seq, LANES), F32),
                        pltpu.VMEM((N_GROUPS, seq, LANES), F32)],
        compiler_params=_cparams(2, 48),
        name="attn_prompt",
    )(*([qkv] * 9), tabs)


ATT_HB = SUBLANES
ATT_PAIRS = ATT_HB // 2


def _sample_attn_unit(q_refs, c_refs, t_refs, tn_refs):
    lane = lax.broadcasted_iota(jnp.int32, (HEAD_DIM, LANES), 1)
    pad = jnp.zeros((LANES - 3 * ATT_PAIRS, LANES), F32)
    heads = range(ATT_HB)
    stack = lambda per_head: jnp.concatenate(per_head, axis=0)
    groups = tuple(zip(q_refs, c_refs, t_refs))

    cols, s_new, scores, chunks = [], [], [], []
    for g, (q_ref, c_ref, t_ref) in enumerate(groups):
        qt = jnp.concatenate([q_ref[...], pad], axis=0).T
        col = [[jnp.broadcast_to(qt[(hl % 2) * HEAD_DIM:(hl % 2 + 1) * HEAD_DIM,
                                    c * ATT_PAIRS + hl // 2:c * ATT_PAIRS + hl // 2 + 1],
                                 (HEAD_DIM, LANES)) for hl in heads] for c in range(3)]
        cols.append(col)
        cks = [slice(k, k + LANES) for k in range(0, c_ref.shape[-1], LANES)]
        chunks.append(cks)
        s_new.append(stack([jnp.sum(col[0][hl] * col[1][hl], axis=0, keepdims=True) for hl in heads])
                     + tn_refs[g][...])
        scores.append([stack([jnp.sum(c_ref[0, hl, :, ck] * col[0][hl], axis=0, keepdims=True)
                              for hl in heads]) + t_ref[:, ck] for ck in cks])

    probs, p_new, dens, lses = [], [], [], []
    for g in range(N_GROUPS):
        mx = jnp.max(functools.reduce(jnp.maximum, scores[g]), axis=-1, keepdims=True)
        mx = jnp.maximum(mx, s_new[g])
        p = [jnp.exp(sc - mx) for sc in scores[g]]
        pn = jnp.exp(s_new[g] - mx)
        den = jnp.sum(functools.reduce(jnp.add, p), axis=-1, keepdims=True) + pn
        probs.append(p)
        p_new.append(pn)
        dens.append(den)
        lses.append(mx + jnp.log(den))

    lmx = jnp.maximum(jnp.maximum(lses[0], lses[1]), lses[2])
    es = [jnp.exp(l - lmx) for l in lses]
    tot = es[0] + es[1] + es[2]
    coef = [e / (tot * dn) for e, dn in zip(es, dens)]

    halves = [jnp.zeros((HEAD_DIM, LANES), F32) for _ in range(2)]
    for hl in heads:
        o = None
        for g, (_, c_ref, _) in enumerate(groups):
            acc = functools.reduce(jnp.add, [c_ref[1, hl, :, ck] * p[hl:hl + 1, :]
                                             for ck, p in zip(chunks[g], probs[g])])
            un = jnp.sum(acc, axis=-1, keepdims=True) + p_new[g][hl:hl + 1, :] * cols[g][2][hl]
            o = coef[g][hl:hl + 1, :] * un if o is None else o + coef[g][hl:hl + 1, :] * un
        halves[hl % 2] = jnp.where(lane == hl // 2, o, halves[hl % 2])
    return jnp.concatenate(halves, axis=0).T[0:ATT_PAIRS, :]


ATT_NHB = N_HEADS // ATT_HB
HOST_UNITS = D_FF // MLP_FC
HOST_ROWS = HOST_UNITS // ATT_NHB
assert HOST_UNITS % 2 == 0 and HOST_UNITS % ATT_NHB == 0


def _mlp_attn_kernel(layer, b_start, x_ref, g_ref, win_ref, wout_ref, q_ref, t0_ref, t1_ref, t2_ref,
                     tn_ref, c0_hbm, c1_hbm, c2_hbm, o_ref, ao_ref, buf0, buf1, buf2, sems):
    i = pl.program_id(0)
    n_steps = pl.num_programs(0)
    bufs = (buf0, buf1, buf2)
    t_refs = (t0_ref, t1_ref, t2_ref)

    def slab_copies(k, step, slot):
        b = b_start + step * HOST_ROWS + k // ATT_NHB
        heads = pl.ds((k % ATT_NHB) * ATT_HB, ATT_HB)
        return [pltpu.make_async_copy(c_hbm.at[layer, b, :, heads], buf.at[slot], sems.at[g, slot])
                for g, (c_hbm, buf) in enumerate(zip((c0_hbm, c1_hbm, c2_hbm), bufs))]

    @pl.when(i == 0)
    def _():
        for cp in slab_copies(0, 0, 0):
            cp.start()

    x = x_ref[...]
    h = _rms(x, g_ref[...]).astype(BF16)
    acc = x
    for k in range(HOST_UNITS):
        slot = k % 2
        for cp in slab_copies(k, i, slot):
            cp.wait()
        if k + 1 < HOST_UNITS:
            for cp in slab_copies(k + 1, i, 1 - slot):
                cp.start()
        else:
            @pl.when(i + 1 < n_steps)
            def _():
                for cp in slab_copies(0, i + 1, 1 - slot):
                    cp.start()
        bl, hb = k // ATT_NHB, k % ATT_NHB
        ao_ref[bl, hb] = _sample_attn_unit(
            [q_ref.at[g, bl, hb] for g in range(N_GROUPS)],
            [buf.at[slot] for buf in bufs],
            [t_ref.at[hb] for t_ref in t_refs],
            [tn_ref.at[g, hb] for g in range(N_GROUPS)])
        c0 = k * MLP_FC
        u = jnp.maximum(_dot(h, win_ref[:, c0:c0 + MLP_FC]), 0.0)
        acc = acc + _dot((u * u).astype(BF16), wout_ref[c0:c0 + MLP_FC, :])
    o_ref[...] = acc


def _mlp_attn(x, g, w_in, w_out, w_layer, qkv_s, caches_t, cache_layer, tabs, tab_new, b_start):
    m = x.shape[0]
    tm = _row_tile(m)
    steps = m // tm
    db = qkv_s.shape[0]
    n_rows = steps * HOST_ROWS
    assert b_start % HOST_ROWS == 0 and b_start + n_rows <= db
    q = qkv_s.reshape(db, N_GROUPS, 3, ATT_NHB, ATT_PAIRS, LANES).transpose(1, 0, 3, 2, 4, 5)
    q = q.reshape(N_GROUPS, db, ATT_NHB, 3 * ATT_PAIRS, LANES)
    row = pl.BlockSpec((tm, D_MODEL), lambda i: (i, 0))
    tspecs = [_resident((ATT_NHB, ATT_HB, w)) for w in WINDOWS]
    slabs = [pltpu.VMEM((2, 2, ATT_HB, HEAD_DIM, w), F32) for w in WINDOWS]
    xo, ao = pl.pallas_call(
        functools.partial(_mlp_attn_kernel, cache_layer, b_start),
        grid=(steps,),
        in_specs=[row, _resident((1, D_MODEL)),
                  _layer_slab((D_MODEL, D_FF), w_layer), _layer_slab((D_FF, D_MODEL), w_layer),
                  pl.BlockSpec((N_GROUPS, HOST_ROWS, ATT_NHB, 3 * ATT_PAIRS, LANES),
                               lambda i: (0, b_start // HOST_ROWS + i, 0, 0, 0)),
                  *tspecs, _resident((N_GROUPS, ATT_NHB, ATT_HB, LANES)),
                  *[pl.BlockSpec(memory_space=pl.ANY)] * N_GROUPS],
        out_specs=[row, pl.BlockSpec((HOST_ROWS, ATT_NHB, ATT_PAIRS, LANES), lambda i: (i, 0, 0, 0))],
        out_shape=[jax.ShapeDtypeStruct((m, D_MODEL), F32),
                   jax.ShapeDtypeStruct((n_rows, ATT_NHB, ATT_PAIRS, LANES), F32)],
        scratch_shapes=[*slabs, pltpu.SemaphoreType.DMA((N_GROUPS, 2))],
        compiler_params=_cparams(1, 58),
        name="mlp_attn",
    )(x, g, w_in, w_out, q, *[t.reshape(ATT_NHB, ATT_HB, -1) for t in tabs],
      tab_new.reshape(N_GROUPS, ATT_NHB, ATT_HB, LANES), *caches_t)
    return xo, ao.reshape(n_rows, D_MODEL)


def _proj_kernel(o_ref, x_ref, w_ref, out_ref):
    out_ref[...] = x_ref[...] + _dot(o_ref[...].astype(BF16), w_ref[...])


def _proj_residual(o, x, w, layer):
    m = x.shape[0]
    tm = _row_tile(m, 1024)
    row = pl.BlockSpec((tm, D_MODEL), lambda i: (i, 0))
    return pl.pallas_call(
        _proj_kernel,
        grid=(m // tm,),
        in_specs=[row, row, _layer_slab((D_MODEL, D_MODEL), layer)],
        out_specs=row,
        out_shape=jax.ShapeDtypeStruct((m, D_MODEL), F32),
        compiler_params=_cparams(1, 40),
        name="proj_residual",
    )(o, x, w)


def _norm_kernel(x_ref, g_ref, o_ref):
    o_ref[...] = _rms(x_ref[...], g_ref[...])


def _final_norm(x, g):
    m = x.shape[0]
    tm = _row_tile(m, 1024)
    return pl.pallas_call(
        _norm_kernel,
        grid=(m // tm,),
        in_specs=[pl.BlockSpec((tm, D_MODEL), lambda i: (i, 0)), _resident((1, D_MODEL))],
        out_specs=pl.BlockSpec((tm, D_MODEL), lambda i: (i, 0)),
        out_shape=jax.ShapeDtypeStruct((m, D_MODEL), F32),
        compiler_params=_cparams(1, 32),
        name="final_norm",
    )(x, g)


def _t5_bucket(dist):
    max_exact = NUM_BUCKETS // 2
    n = jnp.maximum(dist.astype(F32), 1.0)
    large = max_exact + (jnp.log(n / max_exact) / math.log(MAX_DISTANCE / max_exact)
                         * (NUM_BUCKETS - max_exact)).astype(jnp.int32)
    large = jnp.minimum(large, NUM_BUCKETS - 1)
    return jnp.where(dist < max_exact, dist, large)


def _toeplitz(w):
    n = 2 * BLK
    tiled = jnp.tile(w, (1,) * (w.ndim - 1) + (BLK,))[..., :BLK * (n - 1)]
    return tiled.reshape(w.shape[:-1] + (BLK, n - 1))[..., :BLK]


def _bias_tables(rel_bias):
    neg_run = jnp.full((N_HEADS, BLK - 1), NEG, F32)
    neg_one = jnp.full((N_HEADS, 1), NEG, F32)
    prompt, tab, tab_new = [], [], []
    for g in range(N_GROUPS):
        d = DILATIONS[g]
        dist = jnp.arange(BLK + 1, dtype=jnp.int32) * d
        tbl = rel_bias[:, g * N_HEADS:(g + 1) * N_HEADS].astype(F32)
        bias = tbl[_t5_bucket(dist)].T
        rev = bias[:, ::-1]
        w_cur = jnp.concatenate([bias[:, 0:1], neg_run, neg_one, rev[:, 1:BLK]], axis=1)
        w_prev = jnp.concatenate([rev[:, 0:BLK], neg_one, neg_run], axis=1)
        t = jnp.concatenate([_toeplitz(w_prev), _toeplitz(w_cur)], axis=-1)
        prompt.append(t.reshape(N_PAIRS, 2 * BLK, 2 * BLK))
        on_grid = rev[:, 0:BLK, None]
        off_grid = jnp.full((N_HEADS, BLK, d - 1), NEG, F32)
        tab.append(jnp.concatenate([on_grid, off_grid], axis=2).reshape(N_HEADS, BLK * d))
        tab_new.append(jnp.broadcast_to(bias[:, 0:1], (N_HEADS, LANES)))
    return jnp.stack(prompt), tab, jnp.stack(tab_new)


def kernel(x_prompt, x_sample, state_conv, cache_kv_w128, cache_kv_w512, cache_kv_w2048,
           norm_mix, norm_mlp, norm_final,
           conv_w_pw1, conv_b_pw1, conv_w_dw, conv_b_dw, conv_ln_g, conv_ln_b, conv_w_pw2, conv_b_pw2,
           attn_w_qkv, attn_w_o, rel_bias, mlp_w_in, mlp_w_out):
    batch, seq, _ = x_prompt.shape
    db = x_sample.shape[0]
    assert x_sample.shape[1] == 1 and seq % (DILATIONS[-1] * BLK) == 0
    caches = (cache_kv_w128, cache_kv_w512, cache_kv_w2048)
    assert all(c.shape[2] == w for c, w in zip(caches, WINDOWS))

    xp = x_prompt.reshape(batch * seq, D_MODEL)
    xs = x_sample.reshape(db, D_MODEL)
    row = lambda v: v.reshape(1, -1)

    tabs_p, tabs_s, tab_new = _bias_tables(rel_bias)
    caches_t = [jnp.transpose(c, (0, 1, 3, 4, 5, 2)) for c in caches]
    state_t = jnp.transpose(state_conv, (0, 2, 1, 3))

    w_pw1, w_pw2 = conv_w_pw1.astype(BF16), conv_w_pw2.astype(BF16)
    w_qkv, w_o = attn_w_qkv.astype(BF16), attn_w_o.astype(BF16)
    w_in, w_out = mlp_w_in.astype(BF16), mlp_w_out.astype(BF16)

    conv_p, conv_s = [], []
    kv_bufs = ()
    kv_s = [[] for _ in range(N_GROUPS)]
    half = db // 2
    assert half == (batch * seq // _row_tile(batch * seq)) * HOST_ROWS
    for j in range(DEPTH // 2):
        ic, ia = 2 * j, 2 * j + 1
        g_mix = row(norm_mix[ic])
        prm = (j, conv_w_dw[j], row(conv_b_dw[j]), row(conv_ln_g[j]), row(conv_ln_b[j]),
               w_pw2, row(conv_b_pw2[j]))
        a_p = _pw1_glu(xp, g_mix, w_pw1, row(conv_b_pw1[j]), j)
        a_s = _pw1_glu(xs, g_mix, w_pw1, row(conv_b_pw1[j]), j)
        xp = _conv_prompt(a_p, xp, batch, seq, *prm)
        xs, ns = _conv_sample(a_s, state_t, xs, *prm)
        conv_p.append(a_p.reshape(batch, seq, D_MODEL)[:, seq - HIST:])
        conv_s.append(jnp.transpose(ns, (1, 0, 2)))
        xs = _mlp(xs, row(norm_mlp[ic]), w_in, w_out, ic)
        g_mix = row(norm_mix[ia])
        (qkv_s,) = _qkv_proj(xs, g_mix, w_qkv, j)
        for g in range(N_GROUPS):
            kv_new = qkv_s[:, (g * 3 + 1) * D_MODEL:(g * 3 + 3) * D_MODEL]
            kv_s[g].append(kv_new.reshape(db, 1, 2, N_HEADS, HEAD_DIM))
        att = (qkv_s, caches_t, j, tabs_s, tab_new)
        xp, o_lo = _mlp_attn(xp, row(norm_mlp[ic]), w_in, w_out, ic, *att, 0)
        qkv_p, *kv_bufs = _qkv_proj(xp, g_mix, w_qkv, j, batch=batch, seq=seq, kv_bufs=kv_bufs)
        xp = _proj_residual(_attn_prompt(qkv_p, tabs_p, batch, seq), xp, w_o, j)
        xp, o_hi = _mlp_attn(xp, row(norm_mlp[ia]), w_in, w_out, ia, *att, half)
        xs = _proj_residual(jnp.concatenate([o_lo, o_hi], axis=0), xs, w_o, j)
        xs = _mlp(xs, row(norm_mlp[ia]), w_in, w_out, ia)

    y_prompt = _final_norm(xp, row(norm_final)).reshape(batch, seq, D_MODEL)
    y_sample = _final_norm(xs, row(norm_final)).reshape(db, 1, D_MODEL)
    kv_p = [b.reshape(b.shape[:3] + (2, N_HEADS, HEAD_DIM)) for b in kv_bufs]
    return (y_prompt, y_sample, jnp.stack(conv_p), jnp.stack(conv_s),
            kv_p[0], jnp.stack(kv_s[0]), kv_p[1], jnp.stack(kv_s[1]), kv_p[2], jnp.stack(kv_s[2]))
```

```python
import functools
import math

import jax
import jax.numpy as jnp
from jax import lax
from jax.experimental import pallas as pl
from jax.experimental.pallas import tpu as pltpu

F32 = jnp.float32
BF16 = jnp.bfloat16

D_MODEL = 1024
DEPTH = 4
HEAD_DIM = 64
N_HEADS = D_MODEL // HEAD_DIM
WINDOWS = (128, 512, 2048)
DILATIONS = (1, 4, 16)
N_GROUPS = 3
BLK = 128
SCALE = HEAD_DIM ** -0.5
NUM_BUCKETS = 32
MAX_DISTANCE = 2048
CONV_WIDTH = 31
HIST = CONV_WIDTH - 1
D_FF = 4 * D_MODEL
EPS = 1e-6
NEG = -1e30

LANES = 128
SUBLANES = 8
N_PAIRS = D_MODEL // LANES
HALO = 32
MIB = 1 << 20

assert all(w // d == BLK for w, d in zip(WINDOWS, DILATIONS))
assert 2 * HEAD_DIM == LANES


def _resident(shape):
    nd = len(shape)
    return pl.BlockSpec(shape, lambda *_: (0,) * nd, pipeline_mode=pl.Buffered(1))


def _layer_slab(shape, layer):
    nd = len(shape)
    return pl.BlockSpec((None,) + tuple(shape), lambda *_: (layer,) + (0,) * nd, pipeline_mode=pl.Buffered(1))


def _cparams(n_axes, vmem_mib):
    return pltpu.CompilerParams(dimension_semantics=("arbitrary",) * n_axes,
                                vmem_limit_bytes=vmem_mib * MIB)


def _dot(a, b):
    return jnp.dot(a, b, preferred_element_type=F32)


def _dot_nt(a, b):
    return lax.dot_general(a, b, (((1,), (1,)), ((), ())), preferred_element_type=F32)


def _rms(x, g):
    return x * lax.rsqrt(jnp.mean(x * x, axis=-1, keepdims=True) + EPS) * g


def _sigmoid(x):
    return 1.0 / (1.0 + jnp.exp(-x))


def _row_tile(m, tile=512):
    return tile if m % tile == 0 else m


def _pw1_kernel(x_ref, g_ref, w_ref, b_ref, a_ref):
    h = _rms(x_ref[...], g_ref[...]).astype(BF16)
    u = _dot(h, w_ref[...]) + b_ref[...]
    a_ref[...] = u[:, :D_MODEL] * _sigmoid(u[:, D_MODEL:])


def _pw1_glu(x, g, w, b, layer):
    m = x.shape[0]
    tm = _row_tile(m)
    return pl.pallas_call(
        _pw1_kernel,
        grid=(m // tm,),
        in_specs=[pl.BlockSpec((tm, D_MODEL), lambda i: (i, 0)),
                  _resident((1, D_MODEL)),
                  _layer_slab((D_MODEL, 2 * D_MODEL), layer),
                  _resident((1, 2 * D_MODEL))],
        out_specs=pl.BlockSpec((tm, D_MODEL), lambda i: (i, 0)),
        out_shape=jax.ShapeDtypeStruct((m, D_MODEL), F32),
        compiler_params=_cparams(1, 40),
        name="pw1_glu",
    )(x, g, w, b)


CONV_TS = 256
CONV_RC = 64
CONV_CC = 256


def _ln_swish_pw2(c, lng, lnb, w2, b2):
    mu = jnp.mean(c, axis=-1, keepdims=True)
    cc = c - mu
    var = jnp.mean(cc * cc, axis=-1, keepdims=True)
    z = cc * lax.rsqrt(var + EPS) * lng + lnb
    z = z * _sigmoid(z)
    return _dot(z.astype(BF16), w2) + b2


def _conv_prompt_kernel(a_ref, halo_ref, x_ref, wdw_ref, bdw_ref, lng_ref, lnb_ref,
                        w2_ref, b2_ref, o_ref, full_ref, c_ref):
    i = pl.program_id(1)
    full_ref[0:HALO, :] = jnp.where(i > 0, halo_ref[...], 0.0)
    full_ref[HALO:, :] = a_ref[...]
    off = HALO - HIST
    for r0 in range(0, CONV_TS, CONV_RC):
        for c0 in range(0, D_MODEL, CONV_CC):
            cols = slice(c0, c0 + CONV_CC)
            acc = jnp.broadcast_to(bdw_ref[:, cols], (CONV_RC, CONV_CC))
            for phase in range(SUBLANES):
                part = None
                n_rows = CONV_RC + (SUBLANES if phase else 0)
                for k in range(CONV_WIDTH):
                    if (off + k) % SUBLANES != phase:
                        continue
                    base = r0 + off + k - phase
                    term = wdw_ref[k:k + 1, cols] * full_ref[base:base + n_rows, cols]
                    part = term if part is None else part + term
                acc = acc + part[phase:phase + CONV_RC]
            c_ref[r0:r0 + CONV_RC, cols] = acc
    y = _ln_swish_pw2(c_ref[...], lng_ref[...], lnb_ref[...], w2_ref[...], b2_ref[...])
    o_ref[...] = x_ref[...] + y


def _conv_prompt(a, x, batch, seq, layer, wdw, bdw, lng, lnb, w2, b2):
    m = batch * seq
    nt = seq // CONV_TS
    hpt = CONV_TS // HALO
    row = lambda b, i: (b * nt + i, 0)
    return pl.pallas_call(
        _conv_prompt_kernel,
        grid=(batch, nt),
        in_specs=[pl.BlockSpec((CONV_TS, D_MODEL), row),
                  pl.BlockSpec((HALO, D_MODEL),
                               lambda b, i: (jnp.maximum((b * nt + i) * hpt - 1, 0), 0)),
                  pl.BlockSpec((CONV_TS, D_MODEL), row),
                  _resident((CONV_WIDTH, D_MODEL)),
                  _resident((1, D_MODEL)), _resident((1, D_MODEL)), _resident((1, D_MODEL)),
                  _layer_slab((D_MODEL, D_MODEL), layer), _resident((1, D_MODEL))],
        out_specs=pl.BlockSpec((CONV_TS, D_MODEL), row),
        out_shape=jax.ShapeDtypeStruct((m, D_MODEL), F32),
        scratch_shapes=[pltpu.VMEM((CONV_TS + HALO, D_MODEL), F32),
                        pltpu.VMEM((CONV_TS, D_MODEL), F32)],
        compiler_params=_cparams(2, 32),
        name="conv_prompt",
    )(a, a, x, wdw, bdw, lng, lnb, w2, b2)


CONV_SB = 32


def _conv_sample_kernel(a_ref, st_ref, x_ref, wdw_ref, bdw_ref, lng_ref, lnb_ref,
                        w2_ref, b2_ref, o_ref, ns_ref):
    a = a_ref[...]
    c = a * wdw_ref[HIST:CONV_WIDTH, :] + bdw_ref[...]
    for t in range(HIST):
        st = st_ref[t]
        c = c + st * wdw_ref[t:t + 1, :]
        if t > 0:
            ns_ref[t - 1] = st
    ns_ref[HIST - 1] = a
    y = _ln_swish_pw2(c, lng_ref[...], lnb_ref[...], w2_ref[...], b2_ref[...])
    o_ref[...] = x_ref[...] + y


def _conv_sample(a, state_t, x, layer, wdw, bdw, lng, lnb, w2, b2):
    db = x.shape[0]
    row = lambda i: (i, 0)
    return pl.pallas_call(
        _conv_sample_kernel,
        grid=(db // CONV_SB,),
        in_specs=[pl.BlockSpec((CONV_SB, D_MODEL), row),
                  pl.BlockSpec((None, HIST, CONV_SB, D_MODEL), lambda i: (layer, 0, i, 0)),
                  pl.BlockSpec((CONV_SB, D_MODEL), row),
                  _resident((CONV_WIDTH, D_MODEL)),
                  _resident((1, D_MODEL)), _resident((1, D_MODEL)), _resident((1, D_MODEL)),
                  _layer_slab((D_MODEL, D_MODEL), layer), _resident((1, D_MODEL))],
        out_specs=[pl.BlockSpec((CONV_SB, D_MODEL), row),
                   pl.BlockSpec((HIST, CONV_SB, D_MODEL), lambda i: (0, i, 0))],
        out_shape=[jax.ShapeDtypeStruct((db, D_MODEL), F32),
                   jax.ShapeDtypeStruct((HIST, db, D_MODEL), F32)],
        compiler_params=_cparams(1, 40),
        name="conv_sample",
    )(a, state_t, x, wdw, bdw, lng, lnb, w2, b2)


MLP_FC = 1024


def _mlp_kernel(x_ref, g_ref, win_ref, wout_ref, o_ref):
    x = x_ref[...]
    h = _rms(x, g_ref[...]).astype(BF16)
    acc = x
    for c0 in range(0, D_FF, MLP_FC):
        u = jnp.maximum(_dot(h, win_ref[:, c0:c0 + MLP_FC]), 0.0)
        acc = acc + _dot((u * u).astype(BF16), wout_ref[c0:c0 + MLP_FC, :])
    o_ref[...] = acc


def _mlp(x, g, w_in, w_out, layer):
    m = x.shape[0]
    tm = _row_tile(m)
    return pl.pallas_call(
        _mlp_kernel,
        grid=(m // tm,),
        in_specs=[pl.BlockSpec((tm, D_MODEL), lambda i: (i, 0)),
                  _resident((1, D_MODEL)),
                  _layer_slab((D_MODEL, D_FF), layer),
                  _layer_slab((D_FF, D_MODEL), layer)],
        out_specs=pl.BlockSpec((tm, D_MODEL), lambda i: (i, 0)),
        out_shape=jax.ShapeDtypeStruct((m, D_MODEL), F32),
        compiler_params=_cparams(1, 48),
        name="mlp",
    )(x, g, w_in, w_out)


QKV_TM = 256
QKV_COLS = N_GROUPS * 3 * D_MODEL


def _qkv_kernel(keep_rows, n_aliased, x_ref, g_ref, w_ref, *refs):
    qkv_ref, *kv_refs = refs[n_aliased:]
    h = _rms(x_ref[...], g_ref[...]).astype(BF16)
    tm = x_ref.shape[0]
    for g in range(N_GROUPS):
        for c in range(3):
            col = (g * 3 + c) * D_MODEL
            r = _dot(h, w_ref[:, col:col + D_MODEL])
            if c == 0:
                r = r * SCALE
            qkv_ref[:, col:col + D_MODEL] = r.astype(qkv_ref.dtype)
            if kv_refs and c > 0:
                kv_refs[g][:, (c - 1) * D_MODEL:c * D_MODEL] = r[tm - keep_rows[g]:, :]


def _qkv_proj(x, g, w, layer, *, batch=None, seq=None, kv_bufs=()):
    m = x.shape[0]
    prompt = batch is not None
    tm = QKV_TM if prompt else m
    in_specs = [pl.BlockSpec((tm, D_MODEL), lambda i: (i, 0)),
                _resident((1, D_MODEL)),
                _layer_slab((D_MODEL, QKV_COLS), layer)]
    out_specs = [pl.BlockSpec((tm, QKV_COLS), lambda i: (i, 0))]
    out_shape = [jax.ShapeDtypeStruct((m, QKV_COLS), BF16 if prompt else F32)]
    keep_rows = ()
    if prompt:
        tpb = seq // tm
        for win in WINDOWS:
            keep = min(win, seq)
            rows = min(keep, tm)
            first = tpb - keep // rows
            keep_rows += (rows,)
            out_specs.append(pl.BlockSpec(
                (None, None, rows, 2 * D_MODEL),
                lambda i, first=first: (layer, i // tpb, jnp.maximum(i % tpb - first, 0), 0)))
            out_shape.append(jax.ShapeDtypeStruct((DEPTH // 2, batch, keep, 2 * D_MODEL), F32))
    in_specs += [pl.BlockSpec(memory_space=pl.ANY)] * len(kv_bufs)
    return pl.pallas_call(
        functools.partial(_qkv_kernel, keep_rows, len(kv_bufs)),
        grid=(m // tm,),
        in_specs=in_specs, out_specs=out_specs, out_shape=out_shape,
        input_output_aliases={3 + k: 1 + k for k in range(len(kv_bufs))},
        compiler_params=_cparams(1, 56),
        name="qkv_proj",
    )(x, g, w, *kv_bufs)


MERGE_ROWS = 256


def _attn_prompt_kernel(*refs):
    qkv_refs, (tab_ref, o_ref, f_scr, og_scr, lg_scr) = refs[:9], refs[9:]
    seq = o_ref.shape[0]
    lo_q = lax.broadcasted_iota(jnp.int32, (BLK, LANES), 1) < HEAD_DIM
    m_lo, m_hi = {}, {}
    for n_rows in (BLK, 2 * BLK):
        lo = lax.broadcasted_iota(jnp.int32, (n_rows, LANES), 1) < HEAD_DIM
        m_lo[n_rows] = jnp.where(lo, 1.0, 0.0).astype(BF16)
        m_hi[n_rows] = jnp.where(lo, 0.0, 1.0).astype(BF16)

    for g in range(N_GROUPS):
        d = DILATIONS[g]
        if d == 1:
            src = qkv_refs[0:3]
        else:
            src = []
            for c in range(3):
                f_scr[(g - 1) * 3 + c] = qkv_refs[g * 3 + c][...].astype(F32)
                src.append(f_scr.at[(g - 1) * 3 + c])

        def rows(start, d=d):
            return pl.ds(start, BLK) if d == 1 else pl.ds(start, BLK, stride=d)

        def load(c, start, src=src):
            return src[c][rows(start), :].astype(BF16)

        n_blk = seq // d // BLK
        for r in range(d):
            for n in range(n_blk):
                start = r + n * BLK * d
                q = load(0, start)
                kc, vc = load(1, start), load(2, start)
                if n == 0:
                    keys, vals, tab = kc, vc, tab_ref[g, :, BLK:]
                else:
                    prev = start - BLK * d
                    keys = jnp.concatenate([load(1, prev), kc], axis=0)
                    vals = jnp.concatenate([load(2, prev), vc], axis=0)
                    tab = tab_ref[g]
                nk = keys.shape[0]
                q2 = jnp.concatenate([q * m_lo[BLK], q * m_hi[BLK]], axis=0)
                s = _dot_nt(q2, keys) + tab
                mx = jnp.max(s, axis=-1, keepdims=True)
                p = jnp.exp(s - mx).astype(BF16)
                pcat = jnp.concatenate([p[:BLK], p[BLK:]], axis=1)
                rhs = jnp.concatenate(
                    [jnp.concatenate([vals * m_lo[nk], m_lo[nk]], axis=1),
                     jnp.concatenate([vals * m_hi[nk], m_hi[nk]], axis=1)], axis=0)
                res = _dot(pcat, rhs)
                den = res[:, LANES:]
                og_scr[g, rows(start), :] = res[:, :LANES] / den
                lg_scr[g, rows(start), :] = jnp.where(lo_q, mx[:BLK], mx[BLK:]) + jnp.log(den)

    for r0 in range(0, seq, MERGE_ROWS):
        sl = slice(r0, r0 + MERGE_ROWS)
        l0, l1, l2 = lg_scr[0, sl, :], lg_scr[1, sl, :], lg_scr[2, sl, :]
        mx = jnp.maximum(jnp.maximum(l0, l1), l2)
        e0, e1, e2 = jnp.exp(l0 - mx), jnp.exp(l1 - mx), jnp.exp(l2 - mx)
        inv = 1.0 / (e0 + e1 + e2)
        o = (e0 * inv) * og_scr[0, sl, :] + (e1 * inv) * og_scr[1, sl, :] + (e2 * inv) * og_scr[2, sl, :]
        o_ref[sl, :] = o.astype(o_ref.dtype)


def _attn_prompt(qkv, tabs, batch, seq):
    m = batch * seq

    def spec(g, c):
        return pl.BlockSpec((seq, LANES), lambda b, p: (b, (g * 3 + c) * N_PAIRS + p))

    return pl.pallas_call(
        _attn_prompt_kernel,
        grid=(batch, N_PAIRS),
        in_specs=[spec(g, c) for g in range(N_GROUPS) for c in range(3)]
        + [pl.BlockSpec((N_GROUPS, None, 2 * BLK, 2 * BLK), lambda b, p: (0, p, 0, 0))],
        out_specs=pl.BlockSpec((seq, LANES), lambda b, p: (b, p)),
        out_shape=jax.ShapeDtypeStruct((m, D_MODEL), BF16),
        scratch_shapes=[pltpu.VMEM((6, seq, LANES), F32),
                        pltpu.VMEM((N_GROUPS, seq, LANES), F32),
                        pltpu.VMEM((N_GROUPS, seq, LANES), F32)],
        compiler_params=_cparams(2, 48),
        name="attn_prompt",
    )(*([qkv] * 9), tabs)


ATT_HB = SUBLANES
ATT_PAIRS = ATT_HB // 2


def _sample_attn_unit(q_refs, c_refs, t_refs, tn_refs):
    lane = lax.broadcasted_iota(jnp.int32, (HEAD_DIM, LANES), 1)
    pad = jnp.zeros((LANES - 3 * ATT_PAIRS, LANES), F32)
    heads = range(ATT_HB)
    stack = lambda per_head: jnp.concatenate(per_head, axis=0)
    groups = tuple(zip(q_refs, c_refs, t_refs))

    cols, s_new, scores, chunks = [], [], [], []
    for g, (q_ref, c_ref, t_ref) in enumerate(groups):
        qt = jnp.concatenate([q_ref[...], pad], axis=0).T
        col = [[jnp.broadcast_to(qt[(hl % 2) * HEAD_DIM:(hl % 2 + 1) * HEAD_DIM,
                                    c * ATT_PAIRS + hl // 2:c * ATT_PAIRS + hl // 2 + 1],
                                 (HEAD_DIM, LANES)) for hl in heads] for c in range(3)]
        cols.append(col)
        cks = [slice(k, k + LANES) for k in range(0, c_ref.shape[-1], LANES)]
        chunks.append(cks)
        s_new.append(stack([jnp.sum(col[0][hl] * col[1][hl], axis=0, keepdims=True) for hl in heads])
                     + tn_refs[g][...])
        scores.append([stack([jnp.sum(c_ref[0, hl, :, ck] * col[0][hl], axis=0, keepdims=True)
                              for hl in heads]) + t_ref[:, ck] for ck in cks])

    probs, p_new, dens, lses = [], [], [], []
    for g in range(N_GROUPS):
        mx = jnp.max(functools.reduce(jnp.maximum, scores[g]), axis=-1, keepdims=True)
        mx = jnp.maximum(mx, s_new[g])
        p = [jnp.exp(sc - mx) for sc in scores[g]]
        pn = jnp.exp(s_new[g] - mx)
        den = jnp.sum(functools.reduce(jnp.add, p), axis=-1, keepdims=True) + pn
        probs.append(p)
        p_new.append(pn)
        dens.append(den)
        lses.append(mx + jnp.log(den))

    lmx = jnp.maximum(jnp.maximum(lses[0], lses[1]), lses[2])
    es = [jnp.exp(l - lmx) for l in lses]
    tot = es[0] + es[1] + es[2]
    coef = [e / (tot * dn) for e, dn in zip(es, dens)]

    halves = [jnp.zeros((HEAD_DIM, LANES), F32) for _ in range(2)]
    for hl in heads:
        o = None
        for g, (_, c_ref, _) in enumerate(groups):
            acc = functools.reduce(jnp.add, [c_ref[1, hl, :, ck] * p[hl:hl + 1, :]
                                             for ck, p in zip(chunks[g], probs[g])])
            un = jnp.sum(acc, axis=-1, keepdims=True) + p_new[g][hl:hl + 1, :] * cols[g][2][hl]
            o = coef[g][hl:hl + 1, :] * un if o is None else o + coef[g][hl:hl + 1, :] * un
        halves[hl % 2] = jnp.where(lane == hl // 2, o, halves[hl % 2])
    return jnp.concatenate(halves, axis=0).T[0:ATT_PAIRS, :]


ATT_NHB = N_HEADS // ATT_HB
HOST_UNITS = D_FF // MLP_FC
HOST_ROWS = HOST_UNITS // ATT_NHB
assert HOST_UNITS % 2 == 0 and HOST_UNITS % ATT_NHB == 0


def _mlp_attn_kernel(layer, b_start, x_ref, g_ref, win_ref, wout_ref, q_ref, t0_ref, t1_ref, t2_ref,
                     tn_ref, c0_hbm, c1_hbm, c2_hbm, o_ref, ao_ref, buf0, buf1, buf2, sems):
    i = pl.program_id(0)
    n_steps = pl.num_programs(0)
    bufs = (buf0, buf1, buf2)
    t_refs = (t0_ref, t1_ref, t2_ref)

    def slab_copies(k, step, slot):
        b = b_start + step * HOST_ROWS + k // ATT_NHB
        heads = pl.ds((k % ATT_NHB) * ATT_HB, ATT_HB)
        return [pltpu.make_async_copy(c_hbm.at[layer, b, :, heads], buf.at[slot], sems.at[g, slot])
                for g, (c_hbm, buf) in enumerate(zip((c0_hbm, c1_hbm, c2_hbm), bufs))]

    @pl.when(i == 0)
    def _():
        for cp in slab_copies(0, 0, 0):
            cp.start()

    x = x_ref[...]
    h = _rms(x, g_ref[...]).astype(BF16)
    acc = x
    for k in range(HOST_UNITS):
        slot = k % 2
        if k + 1 < HOST_UNITS:
            for cp in slab_copies(k + 1, i, 1 - slot):
                cp.start()
        else:
            @pl.when(i + 1 < n_steps)
            def _():
                for cp in slab_copies(0, i + 1, 1 - slot):
                    cp.start()
        for cp in slab_copies(k, i, slot):
            cp.wait()
        bl, hb = k // ATT_NHB, k % ATT_NHB
        ao_ref[bl, hb] = _sample_attn_unit(
            [q_ref.at[g, bl, hb] for g in range(N_GROUPS)],
            [buf.at[slot] for buf in bufs],
            [t_ref.at[hb] for t_ref in t_refs],
            [tn_ref.at[g, hb] for g in range(N_GROUPS)])
        c0 = k * MLP_FC
        u = jnp.maximum(_dot(h, win_ref[:, c0:c0 + MLP_FC]), 0.0)
        acc = acc + _dot((u * u).astype(BF16), wout_ref[c0:c0 + MLP_FC, :])
    o_ref[...] = acc


def _mlp_attn(x, g, w_in, w_out, w_layer, qkv_s, caches_t, cache_layer, tabs, tab_new, b_start):
    m = x.shape[0]
    tm = _row_tile(m)
    steps = m // tm
    db = qkv_s.shape[0]
    n_rows = steps * HOST_ROWS
    assert b_start % HOST_ROWS == 0 and b_start + n_rows <= db
    q = qkv_s.reshape(db, N_GROUPS, 3, ATT_NHB, ATT_PAIRS, LANES).transpose(1, 0, 3, 2, 4, 5)
    q = q.reshape(N_GROUPS, db, ATT_NHB, 3 * ATT_PAIRS, LANES)
    row = pl.BlockSpec((tm, D_MODEL), lambda i: (i, 0))
    tspecs = [_resident((ATT_NHB, ATT_HB, w)) for w in WINDOWS]
    slabs = [pltpu.VMEM((2, 2, ATT_HB, HEAD_DIM, w), F32) for w in WINDOWS]
    xo, ao = pl.pallas_call(
        functools.partial(_mlp_attn_kernel, cache_layer, b_start),
        grid=(steps,),
        in_specs=[row, _resident((1, D_MODEL)),
                  _layer_slab((D_MODEL, D_FF), w_layer), _layer_slab((D_FF, D_MODEL), w_layer),
                  pl.BlockSpec((N_GROUPS, HOST_ROWS, ATT_NHB, 3 * ATT_PAIRS, LANES),
                               lambda i: (0, b_start // HOST_ROWS + i, 0, 0, 0)),
                  *tspecs, _resident((N_GROUPS, ATT_NHB, ATT_HB, LANES)),
                  *[pl.BlockSpec(memory_space=pl.ANY)] * N_GROUPS],
        out_specs=[row, pl.BlockSpec((HOST_ROWS, ATT_NHB, ATT_PAIRS, LANES), lambda i: (i, 0, 0, 0))],
        out_shape=[jax.ShapeDtypeStruct((m, D_MODEL), F32),
                   jax.ShapeDtypeStruct((n_rows, ATT_NHB, ATT_PAIRS, LANES), F32)],
        scratch_shapes=[*slabs, pltpu.SemaphoreType.DMA((N_GROUPS, 2))],
        compiler_params=_cparams(1, 58),
        name="mlp_attn",
    )(x, g, w_in, w_out, q, *[t.reshape(ATT_NHB, ATT_HB, -1) for t in tabs],
      tab_new.reshape(N_GROUPS, ATT_NHB, ATT_HB, LANES), *caches_t)
    return xo, ao.reshape(n_rows, D_MODEL)


def _proj_kernel(o_ref, x_ref, w_ref, out_ref):
    out_ref[...] = x_ref[...] + _dot(o_ref[...].astype(BF16), w_ref[...])


def _proj_residual(o, x, w, layer):
    m = x.shape[0]
    tm = _row_tile(m, 1024)
    row = pl.BlockSpec((tm, D_MODEL), lambda i: (i, 0))
    return pl.pallas_call(
        _proj_kernel,
        grid=(m // tm,),
        in_specs=[row, row, _layer_slab((D_MODEL, D_MODEL), layer)],
        out_specs=row,
        out_shape=jax.ShapeDtypeStruct((m, D_MODEL), F32),
        compiler_params=_cparams(1, 40),
        name="proj_residual",
    )(o, x, w)


def _norm_kernel(x_ref, g_ref, o_ref):
    o_ref[...] = _rms(x_ref[...], g_ref[...])


def _final_norm(x, g):
    m = x.shape[0]
    tm = _row_tile(m, 1024)
    return pl.pallas_call(
        _norm_kernel,
        grid=(m // tm,),
        in_specs=[pl.BlockSpec((tm, D_MODEL), lambda i: (i, 0)), _resident((1, D_MODEL))],
        out_specs=pl.BlockSpec((tm, D_MODEL), lambda i: (i, 0)),
        out_shape=jax.ShapeDtypeStruct((m, D_MODEL), F32),
        compiler_params=_cparams(1, 32),
        name="final_norm",
    )(x, g)


def _t5_bucket(dist):
    max_exact = NUM_BUCKETS // 2
    n = jnp.maximum(dist.astype(F32), 1.0)
    large = max_exact + (jnp.log(n / max_exact) / math.log(MAX_DISTANCE / max_exact)
                         * (NUM_BUCKETS - max_exact)).astype(jnp.int32)
    large = jnp.minimum(large, NUM_BUCKETS - 1)
    return jnp.where(dist < max_exact, dist, large)


def _toeplitz(w):
    n = 2 * BLK
    tiled = jnp.tile(w, (1,) * (w.ndim - 1) + (BLK,))[..., :BLK * (n - 1)]
    return tiled.reshape(w.shape[:-1] + (BLK, n - 1))[..., :BLK]


def _bias_tables(rel_bias):
    neg_run = jnp.full((N_HEADS, BLK - 1), NEG, F32)
    neg_one = jnp.full((N_HEADS, 1), NEG, F32)
    prompt, tab, tab_new = [], [], []
    for g in range(N_GROUPS):
        d = DILATIONS[g]
        dist = jnp.arange(BLK + 1, dtype=jnp.int32) * d
        tbl = rel_bias[:, g * N_HEADS:(g + 1) * N_HEADS].astype(F32)
        bias = tbl[_t5_bucket(dist)].T
        rev = bias[:, ::-1]
        w_cur = jnp.concatenate([bias[:, 0:1], neg_run, neg_one, rev[:, 1:BLK]], axis=1)
        w_prev = jnp.concatenate([rev[:, 0:BLK], neg_one, neg_run], axis=1)
        t = jnp.concatenate([_toeplitz(w_prev), _toeplitz(w_cur)], axis=-1)
        prompt.append(t.reshape(N_PAIRS, 2 * BLK, 2 * BLK))
        on_grid = rev[:, 0:BLK, None]
        off_grid = jnp.full((N_HEADS, BLK, d - 1), NEG, F32)
        tab.append(jnp.concatenate([on_grid, off_grid], axis=2).reshape(N_HEADS, BLK * d))
        tab_new.append(jnp.broadcast_to(bias[:, 0:1], (N_HEADS, LANES)))
    return jnp.stack(prompt), tab, jnp.stack(tab_new)


def kernel(x_prompt, x_sample, state_conv, cache_kv_w128, cache_kv_w512, cache_kv_w2048,
           norm_mix, norm_mlp, norm_final,
           conv_w_pw1, conv_b_pw1, conv_w_dw, conv_b_dw, conv_ln_g, conv_ln_b, conv_w_pw2, conv_b_pw2,
           attn_w_qkv, attn_w_o, rel_bias, mlp_w_in, mlp_w_out):
    batch, seq, _ = x_prompt.shape
    db = x_sample.shape[0]
    assert x_sample.shape[1] == 1 and seq % (DILATIONS[-1] * BLK) == 0
    caches = (cache_kv_w128, cache_kv_w512, cache_kv_w2048)
    assert all(c.shape[2] == w for c, w in zip(caches, WINDOWS))

    xp = x_prompt.reshape(batch * seq, D_MODEL)
    xs = x_sample.reshape(db, D_MODEL)
    row = lambda v: v.reshape(1, -1)

    tabs_p, tabs_s, tab_new = _bias_tables(rel_bias)
    caches_t = [jnp.transpose(c, (0, 1, 3, 4, 5, 2)) for c in caches]
    state_t = jnp.transpose(state_conv, (0, 2, 1, 3))

    w_pw1, w_pw2 = conv_w_pw1.astype(BF16), conv_w_pw2.astype(BF16)
    w_qkv, w_o = attn_w_qkv.astype(BF16), attn_w_o.astype(BF16)
    w_in, w_out = mlp_w_in.astype(BF16), mlp_w_out.astype(BF16)

    conv_p, conv_s = [], []
    kv_bufs = ()
    kv_s = [[] for _ in range(N_GROUPS)]
    half = db // 2
    assert half == (batch * seq // _row_tile(batch * seq)) * HOST_ROWS
    for j in range(DEPTH // 2):
        ic, ia = 2 * j, 2 * j + 1
        g_mix = row(norm_mix[ic])
        prm = (j, conv_w_dw[j], row(conv_b_dw[j]), row(conv_ln_g[j]), row(conv_ln_b[j]),
               w_pw2, row(conv_b_pw2[j]))
        a_p = _pw1_glu(xp, g_mix, w_pw1, row(conv_b_pw1[j]), j)
        a_s = _pw1_glu(xs, g_mix, w_pw1, row(conv_b_pw1[j]), j)
        xp = _conv_prompt(a_p, xp, batch, seq, *prm)
        xs, ns = _conv_sample(a_s, state_t, xs, *prm)
        conv_p.append(a_p.reshape(batch, seq, D_MODEL)[:, seq - HIST:])
        conv_s.append(jnp.transpose(ns, (1, 0, 2)))
        xs = _mlp(xs, row(norm_mlp[ic]), w_in, w_out, ic)
        g_mix = row(norm_mix[ia])
        (qkv_s,) = _qkv_proj(xs, g_mix, w_qkv, j)
        for g in range(N_GROUPS):
            kv_new = qkv_s[:, (g * 3 + 1) * D_MODEL:(g * 3 + 3) * D_MODEL]
            kv_s[g].append(kv_new.reshape(db, 1, 2, N_HEADS, HEAD_DIM))
        att = (qkv_s, caches_t, j, tabs_s, tab_new)
        xp, o_lo = _mlp_attn(xp, row(norm_mlp[ic]), w_in, w_out, ic, *att, 0)
        qkv_p, *kv_bufs = _qkv_proj(xp, g_mix, w_qkv, j, batch=batch, seq=seq, kv_bufs=kv_bufs)
        xp = _proj_residual(_attn_prompt(qkv_p, tabs_p, batch, seq), xp, w_o, j)
        xp, o_hi = _mlp_attn(xp, row(norm_mlp[ia]), w_in, w_out, ia, *att, half)
        xs = _proj_residual(jnp.concatenate([o_lo, o_hi], axis=0), xs, w_o, j)
        xs = _mlp(xs, row(norm_mlp[ia]), w_in, w_out, ia)

    y_prompt = _final_norm(xp, row(norm_final)).reshape(batch, seq, D_MODEL)
    y_sample = _final_norm(xs, row(norm_final)).reshape(db, 1, D_MODEL)
    kv_p = [b.reshape(b.shape[:3] + (2, N_HEADS, HEAD_DIM)) for b in kv_bufs]
    return (y_prompt, y_sample, jnp.stack(conv_p), jnp.stack(conv_s),
            kv_p[0], jnp.stack(kv_s[0]), kv_p[1], jnp.stack(kv_s[1]), kv_p[2], jnp.stack(kv_s[2]))
```

```python
import functools
import math

import jax
import jax.numpy as jnp
from jax import lax
from jax.experimental import pallas as pl
from jax.experimental.pallas import tpu as pltpu

F32 = jnp.float32
BF16 = jnp.bfloat16

D_MODEL = 1024
DEPTH = 4
HEAD_DIM = 64
N_HEADS = D_MODEL // HEAD_DIM
WINDOWS = (128, 512, 2048)
DILATIONS = (1, 4, 16)
N_GROUPS = 3
BLK = 128
SCALE = HEAD_DIM ** -0.5
NUM_BUCKETS = 32
MAX_DISTANCE = 2048
CONV_WIDTH = 31
HIST = CONV_WIDTH - 1
D_FF = 4 * D_MODEL
EPS = 1e-6
NEG = -1e30

LANES = 128
SUBLANES = 8
N_PAIRS = D_MODEL // LANES
HALO = 32
MIB = 1 << 20

assert all(w // d == BLK for w, d in zip(WINDOWS, DILATIONS))
assert 2 * HEAD_DIM == LANES


def _resident(shape):
    nd = len(shape)
    return pl.BlockSpec(shape, lambda *_: (0,) * nd, pipeline_mode=pl.Buffered(1))


def _layer_slab(shape, layer):
    nd = len(shape)
    return pl.BlockSpec((None,) + tuple(shape), lambda *_: (layer,) + (0,) * nd, pipeline_mode=pl.Buffered(1))


def _cparams(n_axes, vmem_mib):
    return pltpu.CompilerParams(dimension_semantics=("arbitrary",) * n_axes,
                                vmem_limit_bytes=vmem_mib * MIB)


def _dot(a, b):
    return jnp.dot(a, b, preferred_element_type=F32)


def _dot_nt(a, b):
    return lax.dot_general(a, b, (((1,), (1,)), ((), ())), preferred_element_type=F32)


def _rms(x, g):
    return x * lax.rsqrt(jnp.mean(x * x, axis=-1, keepdims=True) + EPS) * g


def _sigmoid(x):
    return 1.0 / (1.0 + jnp.exp(-x))


def _row_tile(m, tile=512):
    return tile if m % tile == 0 else m


def _pw1_kernel(x_ref, g_ref, w_ref, b_ref, a_ref):
    h = _rms(x_ref[...], g_ref[...]).astype(BF16)
    u = _dot(h, w_ref[...]) + b_ref[...]
    a_ref[...] = u[:, :D_MODEL] * _sigmoid(u[:, D_MODEL:])


def _pw1_glu(x, g, w, b, layer):
    m = x.shape[0]
    tm = _row_tile(m)
    return pl.pallas_call(
        _pw1_kernel,
        grid=(m // tm,),
        in_specs=[pl.BlockSpec((tm, D_MODEL), lambda i: (i, 0)),
                  _resident((1, D_MODEL)),
                  _layer_slab((D_MODEL, 2 * D_MODEL), layer),
                  _resident((1, 2 * D_MODEL))],
        out_specs=pl.BlockSpec((tm, D_MODEL), lambda i: (i, 0)),
        out_shape=jax.ShapeDtypeStruct((m, D_MODEL), F32),
        compiler_params=_cparams(1, 40),
        name="pw1_glu",
    )(x, g, w, b)


CONV_TS = 256
CONV_RC = 64
CONV_CC = 256


def _ln_swish_pw2(c, lng, lnb, w2, b2):
    mu = jnp.mean(c, axis=-1, keepdims=True)
    cc = c - mu
    var = jnp.mean(cc * cc, axis=-1, keepdims=True)
    z = cc * lax.rsqrt(var + EPS) * lng + lnb
    z = z * _sigmoid(z)
    return _dot(z.astype(BF16), w2) + b2


def _conv_prompt_kernel(a_ref, halo_ref, x_ref, wdw_ref, bdw_ref, lng_ref, lnb_ref,
                        w2_ref, b2_ref, o_ref, full_ref, c_ref):
    i = pl.program_id(1)
    full_ref[0:HALO, :] = jnp.where(i > 0, halo_ref[...], 0.0)
    full_ref[HALO:, :] = a_ref[...]
    off = HALO - HIST
    for r0 in range(0, CONV_TS, CONV_RC):
        for c0 in range(0, D_MODEL, CONV_CC):
            cols = slice(c0, c0 + CONV_CC)
            acc = jnp.broadcast_to(bdw_ref[:, cols], (CONV_RC, CONV_CC))
            for phase in range(SUBLANES):
                part = None
                n_rows = CONV_RC + (SUBLANES if phase else 0)
                for k in range(CONV_WIDTH):
                    if (off + k) % SUBLANES != phase:
                        continue
                    base = r0 + off + k - phase
                    term = wdw_ref[k:k + 1, cols] * full_ref[base:base + n_rows, cols]
                    part = term if part is None else part + term
                acc = acc + part[phase:phase + CONV_RC]
            c_ref[r0:r0 + CONV_RC, cols] = acc
    y = _ln_swish_pw2(c_ref[...], lng_ref[...], lnb_ref[...], w2_ref[...], b2_ref[...])
    o_ref[...] = x_ref[...] + y


def _conv_prompt(a, x, batch, seq, layer, wdw, bdw, lng, lnb, w2, b2):
    m = batch * seq
    nt = seq // CONV_TS
    hpt = CONV_TS // HALO
    row = lambda b, i: (b * nt + i, 0)
    return pl.pallas_call(
        _conv_prompt_kernel,
        grid=(batch, nt),
        in_specs=[pl.BlockSpec((CONV_TS, D_MODEL), row),
                  pl.BlockSpec((HALO, D_MODEL),
                               lambda b, i: (jnp.maximum((b * nt + i) * hpt - 1, 0), 0)),
                  pl.BlockSpec((CONV_TS, D_MODEL), row),
                  _resident((CONV_WIDTH, D_MODEL)),
                  _resident((1, D_MODEL)), _resident((1, D_MODEL)), _resident((1, D_MODEL)),
                  _layer_slab((D_MODEL, D_MODEL), layer), _resident((1, D_MODEL))],
        out_specs=pl.BlockSpec((CONV_TS, D_MODEL), row),
        out_shape=jax.ShapeDtypeStruct((m, D_MODEL), F32),
        scratch_shapes=[pltpu.VMEM((CONV_TS + HALO, D_MODEL), F32),
                        pltpu.VMEM((CONV_TS, D_MODEL), F32)],
        compiler_params=_cparams(2, 32),
        name="conv_prompt",
    )(a, a, x, wdw, bdw, lng, lnb, w2, b2)


CONV_SB = 32


def _conv_sample_kernel(a_ref, st_ref, x_ref, wdw_ref, bdw_ref, lng_ref, lnb_ref,
                        w2_ref, b2_ref, o_ref, ns_ref):
    a = a_ref[...]
    c = a * wdw_ref[HIST:CONV_WIDTH, :] + bdw_ref[...]
    for t in range(HIST):
        st = st_ref[t]
        c = c + st * wdw_ref[t:t + 1, :]
        if t > 0:
            ns_ref[t - 1] = st
    ns_ref[HIST - 1] = a
    y = _ln_swish_pw2(c, lng_ref[...], lnb_ref[...], w2_ref[...], b2_ref[...])
    o_ref[...] = x_ref[...] + y


def _conv_sample(a, state_t, x, layer, wdw, bdw, lng, lnb, w2, b2):
    db = x.shape[0]
    row = lambda i: (i, 0)
    return pl.pallas_call(
        _conv_sample_kernel,
        grid=(db // CONV_SB,),
        in_specs=[pl.BlockSpec((CONV_SB, D_MODEL), row),
                  pl.BlockSpec((None, HIST, CONV_SB, D_MODEL), lambda i: (layer, 0, i, 0)),
                  pl.BlockSpec((CONV_SB, D_MODEL), row),
                  _resident((CONV_WIDTH, D_MODEL)),
                  _resident((1, D_MODEL)), _resident((1, D_MODEL)), _resident((1, D_MODEL)),
                  _layer_slab((D_MODEL, D_MODEL), layer), _resident((1, D_MODEL))],
        out_specs=[pl.BlockSpec((CONV_SB, D_MODEL), row),
                   pl.BlockSpec((HIST, CONV_SB, D_MODEL), lambda i: (0, i, 0))],
        out_shape=[jax.ShapeDtypeStruct((db, D_MODEL), F32),
                   jax.ShapeDtypeStruct((HIST, db, D_MODEL), F32)],
        compiler_params=_cparams(1, 40),
        name="conv_sample",
    )(a, state_t, x, wdw, bdw, lng, lnb, w2, b2)


MLP_FC = 1024


def _mlp_kernel(x_ref, g_ref, win_ref, wout_ref, o_ref):
    x = x_ref[...]
    h = _rms(x, g_ref[...]).astype(BF16)
    acc = x
    for c0 in range(0, D_FF, MLP_FC):
        u = jnp.maximum(_dot(h, win_ref[:, c0:c0 + MLP_FC]), 0.0)
        acc = acc + _dot((u * u).astype(BF16), wout_ref[c0:c0 + MLP_FC, :])
    o_ref[...] = acc


def _mlp(x, g, w_in, w_out, layer):
    m = x.shape[0]
    tm = _row_tile(m)
    return pl.pallas_call(
        _mlp_kernel,
        grid=(m // tm,),
        in_specs=[pl.BlockSpec((tm, D_MODEL), lambda i: (i, 0)),
                  _resident((1, D_MODEL)),
                  _layer_slab((D_MODEL, D_FF), layer),
                  _layer_slab((D_FF, D_MODEL), layer)],
        out_specs=pl.BlockSpec((tm, D_MODEL), lambda i: (i, 0)),
        out_shape=jax.ShapeDtypeStruct((m, D_MODEL), F32),
        compiler_params=_cparams(1, 48),
        name="mlp",
    )(x, g, w_in, w_out)


QKV_TM = 256
QKV_COLS = N_GROUPS * 3 * D_MODEL


def _qkv_kernel(x_ref, g_ref, w_ref, qkv_ref):
    h = _rms(x_ref[...], g_ref[...]).astype(BF16)
    for col in range(0, QKV_COLS, D_MODEL):
        r = _dot(h, w_ref[:, col:col + D_MODEL])
        if col % (3 * D_MODEL) == 0:
            r = r * SCALE
        qkv_ref[:, col:col + D_MODEL] = r


def _qkv_proj(x, g, w, layer):
    m = x.shape[0]
    return pl.pallas_call(
        _qkv_kernel,
        grid=(1,),
        in_specs=[pl.BlockSpec((m, D_MODEL), lambda i: (0, 0)),
                  _resident((1, D_MODEL)),
                  _layer_slab((D_MODEL, QKV_COLS), layer)],
        out_specs=pl.BlockSpec((m, QKV_COLS), lambda i: (0, 0)),
        out_shape=jax.ShapeDtypeStruct((m, QKV_COLS), F32),
        compiler_params=_cparams(1, 40),
        name="qkv_proj",
    )(x, g, w)


MERGE_ROWS = 256


def _attn_prompt_kernel(*refs):
    qkv_refs, (tab_ref, o_ref, f_scr, og_scr, lg_scr) = refs[:9], refs[9:]
    seq = o_ref.shape[0]
    lo_q = lax.broadcasted_iota(jnp.int32, (BLK, LANES), 1) < HEAD_DIM
    m_lo, m_hi = {}, {}
    for n_rows in (BLK, 2 * BLK):
        lo = lax.broadcasted_iota(jnp.int32, (n_rows, LANES), 1) < HEAD_DIM
        m_lo[n_rows] = jnp.where(lo, 1.0, 0.0).astype(BF16)
        m_hi[n_rows] = jnp.where(lo, 0.0, 1.0).astype(BF16)

    for g in range(N_GROUPS):
        d = DILATIONS[g]
        if d == 1:
            src = qkv_refs[0:3]
        else:
            src = []
            for c in range(3):
                f_scr[(g - 1) * 3 + c] = qkv_refs[g * 3 + c][...].astype(F32)
                src.append(f_scr.at[(g - 1) * 3 + c])

        def rows(start, d=d):
            return pl.ds(start, BLK) if d == 1 else pl.ds(start, BLK, stride=d)

        def load(c, start, src=src):
            return src[c][rows(start), :].astype(BF16)

        n_blk = seq // d // BLK
        for r in range(d):
            for n in range(n_blk):
                start = r + n * BLK * d
                q = load(0, start)
                kc, vc = load(1, start), load(2, start)
                if n == 0:
                    keys, vals, tab = kc, vc, tab_ref[g, :, BLK:]
                else:
                    prev = start - BLK * d
                    keys = jnp.concatenate([load(1, prev), kc], axis=0)
                    vals = jnp.concatenate([load(2, prev), vc], axis=0)
                    tab = tab_ref[g]
                nk = keys.shape[0]
                q2 = jnp.concatenate([q * m_lo[BLK], q * m_hi[BLK]], axis=0)
                s = _dot_nt(q2, keys) + tab
                mx = jnp.max(s, axis=-1, keepdims=True)
                p = jnp.exp(s - mx).astype(BF16)
                pcat = jnp.concatenate([p[:BLK], p[BLK:]], axis=1)
                rhs = jnp.concatenate(
                    [jnp.concatenate([vals * m_lo[nk], m_lo[nk]], axis=1),
                     jnp.concatenate([vals * m_hi[nk], m_hi[nk]], axis=1)], axis=0)
                res = _dot(pcat, rhs)
                den = res[:, LANES:]
                og_scr[g, rows(start), :] = res[:, :LANES] / den
                lg_scr[g, rows(start), :] = jnp.where(lo_q, mx[:BLK], mx[BLK:]) + jnp.log(den)

    for r0 in range(0, seq, MERGE_ROWS):
        sl = slice(r0, r0 + MERGE_ROWS)
        l0, l1, l2 = lg_scr[0, sl, :], lg_scr[1, sl, :], lg_scr[2, sl, :]
        mx = jnp.maximum(jnp.maximum(l0, l1), l2)
        e0, e1, e2 = jnp.exp(l0 - mx), jnp.exp(l1 - mx), jnp.exp(l2 - mx)
        inv = 1.0 / (e0 + e1 + e2)
        o = (e0 * inv) * og_scr[0, sl, :] + (e1 * inv) * og_scr[1, sl, :] + (e2 * inv) * og_scr[2, sl, :]
        o_ref[sl, :] = o.astype(o_ref.dtype)


def _attn_prompt(qkv, tabs, batch, seq):
    m = batch * seq

    def spec(g, c):
        return pl.BlockSpec((seq, LANES), lambda b, p: (b, (g * 3 + c) * N_PAIRS + p))

    return pl.pallas_call(
        _attn_prompt_kernel,
        grid=(batch, N_PAIRS),
        in_specs=[spec(g, c) for g in range(N_GROUPS) for c in range(3)]
        + [pl.BlockSpec((N_GROUPS, None, 2 * BLK, 2 * BLK), lambda b, p: (0, p, 0, 0))],
        out_specs=pl.BlockSpec((seq, LANES), lambda b, p: (b, p)),
        out_shape=jax.ShapeDtypeStruct((m, D_MODEL), BF16),
        scratch_shapes=[pltpu.VMEM((6, seq, LANES), F32),
                        pltpu.VMEM((N_GROUPS, seq, LANES), F32),
                        pltpu.VMEM((N_GROUPS, seq, LANES), F32)],
        compiler_params=_cparams(2, 48),
        name="attn_prompt",
    )(*([qkv] * 9), tabs)


ATT_HB = SUBLANES
ATT_PAIRS = ATT_HB // 2


def _sample_attn_unit(q_refs, c_refs, t_refs, tn_refs):
    lane = lax.broadcasted_iota(jnp.int32, (HEAD_DIM, LANES), 1)
    head_of_lane = 2 * lax.broadcasted_iota(jnp.int32, (ATT_HB, LANES), 1)
    head_row = lax.broadcasted_iota(jnp.int32, (ATT_HB, LANES), 0)
    pad = jnp.zeros((LANES - 3 * ATT_PAIRS, LANES), F32)
    heads = range(ATT_HB)
    stack = lambda per_head: jnp.concatenate(per_head, axis=0)

    unnorm, dens, lses = [], [], []
    for q_ref, c_ref, t_ref, tn_ref in zip(q_refs, c_refs, t_refs, tn_refs):
        qt = jnp.concatenate([q_ref[...], pad], axis=0).T

        def col(c, hl, qt=qt):
            j = c * ATT_PAIRS + hl // 2
            return qt[(hl % 2) * HEAD_DIM:(hl % 2 + 1) * HEAD_DIM, j:j + 1]

        qb = [jnp.broadcast_to(col(0, hl), (HEAD_DIM, LANES)) for hl in heads]
        cks = [slice(k, k + LANES) for k in range(0, c_ref.shape[-1], LANES)]
        s_new = stack([jnp.sum(col(0, hl) * col(1, hl), axis=0, keepdims=True) for hl in heads]) + tn_ref[...]
        scores = [stack([jnp.sum(c_ref[0, hl, :, ck] * qb[hl], axis=0, keepdims=True) for hl in heads])
                  + t_ref[:, ck] for ck in cks]
        mx = jnp.max(functools.reduce(jnp.maximum, scores), axis=-1, keepdims=True)
        mx = jnp.maximum(mx, s_new)
        p = [jnp.exp(sc - mx) for sc in scores]
        pn = jnp.exp(s_new - mx)
        den = jnp.sum(functools.reduce(jnp.add, p), axis=-1, keepdims=True) + pn
        halves = [jnp.zeros((HEAD_DIM, LANES), F32) for _ in range(2)]
        for hl in heads:
            acc = functools.reduce(jnp.add, [c_ref[1, hl, :, ck] * pc[hl:hl + 1, :]
                                             for ck, pc in zip(cks, p)])
            un = jnp.sum(acc, axis=-1, keepdims=True) + pn[hl:hl + 1, 0:1] * col(2, hl)
            halves[hl % 2] = jnp.where(lane == hl // 2, un, halves[hl % 2])
        unnorm.append(halves)
        dens.append(den)
        lses.append(mx + jnp.log(den))

    lmx = jnp.maximum(jnp.maximum(lses[0], lses[1]), lses[2])
    es = [jnp.exp(l - lmx) for l in lses]
    tot = es[0] + es[1] + es[2]
    coef = [e / (tot * dn) for e, dn in zip(es, dens)]
    out = []
    for parity in range(2):
        o = None
        for g in range(N_GROUPS):
            c_row = jnp.sum(jnp.where(head_row == head_of_lane + parity, coef[g], 0.0), axis=0, keepdims=True)
            o = unnorm[g][parity] * c_row if o is None else o + unnorm[g][parity] * c_row
        out.append(o)
    return jnp.concatenate(out, axis=0).T[0:ATT_PAIRS, :]


ATT_NHB = N_HEADS // ATT_HB
MLP_UNITS = 3
MLP_CHUNKS = D_FF // MLP_FC
assert MLP_UNITS <= MLP_CHUNKS


def _slab_copies(layer, u, c_hbms, bufs, sems):
    heads = pl.ds((u % ATT_NHB) * ATT_HB, ATT_HB)
    return [pltpu.make_async_copy(c_hbm.at[layer, u // ATT_NHB, :, heads], buf.at[u % 2], sems.at[g, u % 2])
            for g, (c_hbm, buf) in enumerate(zip(c_hbms, bufs))]


def _carry_unit(layer, u_first, u_end, q_ref, t_refs, tn_ref, c_hbms, ao_ref, bufs, sems, u):
    @pl.when(u + 1 < u_end)
    def _():
        for cp in _slab_copies(layer, u + 1, c_hbms, bufs, sems):
            cp.start()

    for cp in _slab_copies(layer, u, c_hbms, bufs, sems):
        cp.wait()
    bl, hb, slot = u // ATT_NHB - u_first // ATT_NHB, u % ATT_NHB, u % 2
    ao_ref[bl, hb] = _sample_attn_unit(
        [q_ref.at[g, bl, hb] for g in range(N_GROUPS)],
        [buf.at[slot] for buf in bufs],
        [t_ref.at[hb] for t_ref in t_refs],
        [tn_ref.at[g, hb] for g in range(N_GROUPS)])


def _attn_operands(qkv_s, caches_t, tabs, tab_new, u_first, n_units):
    db = qkv_s.shape[0]
    assert u_first % ATT_NHB == 0 and n_units % ATT_NHB == 0
    b0, rows = u_first // ATT_NHB, n_units // ATT_NHB
    q = qkv_s[b0:b0 + rows].reshape(rows, N_GROUPS, 3, ATT_NHB, ATT_PAIRS, LANES).transpose(1, 0, 3, 2, 4, 5)
    q = q.reshape(N_GROUPS, rows, ATT_NHB, 3 * ATT_PAIRS, LANES)
    args = [q, *[t.reshape(ATT_NHB, ATT_HB, -1) for t in tabs],
            tab_new.reshape(N_GROUPS, ATT_NHB, ATT_HB, LANES), *caches_t]
    specs = [_resident(q.shape), *[_resident((ATT_NHB, ATT_HB, w)) for w in WINDOWS],
             _resident((N_GROUPS, ATT_NHB, ATT_HB, LANES)), *[pl.BlockSpec(memory_space=pl.ANY)] * N_GROUPS]
    out_shape = jax.ShapeDtypeStruct((rows, ATT_NHB, ATT_PAIRS, LANES), F32)
    scratch = [*[pltpu.VMEM((2, 2, ATT_HB, HEAD_DIM, w), F32) for w in WINDOWS],
               pltpu.SemaphoreType.DMA((N_GROUPS, 2))]
    return args, specs, out_shape, scratch


def _mlp_attn_kernel(layer, u_first, u_end, x_ref, g_ref, win_ref, wout_ref, q_ref, t0_ref, t1_ref, t2_ref,
                     tn_ref, c0_hbm, c1_hbm, c2_hbm, o_ref, ao_ref, buf0, buf1, buf2, sems):
    i = pl.program_id(0)
    c_hbms, bufs = (c0_hbm, c1_hbm, c2_hbm), (buf0, buf1, buf2)
    carry = functools.partial(_carry_unit, layer, u_first, u_end, q_ref, (t0_ref, t1_ref, t2_ref), tn_ref,
                              c_hbms, ao_ref, bufs, sems)

    @pl.when(i == 0)
    def _():
        for cp in _slab_copies(layer, u_first, c_hbms, bufs, sems):
            cp.start()

    x = x_ref[...]
    h = _rms(x, g_ref[...]).astype(BF16)
    acc = x
    for k in range(MLP_CHUNKS):
        c0 = k * MLP_FC
        u = jnp.maximum(_dot(h, win_ref[:, c0:c0 + MLP_FC]), 0.0)
        acc = acc + _dot((u * u).astype(BF16), wout_ref[c0:c0 + MLP_FC, :])
        if k < MLP_UNITS:
            carry(u_first + i * MLP_UNITS + k)
    o_ref[...] = acc


def _mlp_attn(x, g, w_in, w_out, w_layer, qkv_s, caches_t, cache_layer, tabs, tab_new, u_first):
    m = x.shape[0]
    tm = _row_tile(m)
    n_units = (m // tm) * MLP_UNITS
    a_args, a_specs, a_shape, a_scratch = _attn_operands(qkv_s, caches_t, tabs, tab_new, u_first, n_units)
    row = pl.BlockSpec((tm, D_MODEL), lambda i: (i, 0))
    xo, ao = pl.pallas_call(
        functools.partial(_mlp_attn_kernel, cache_layer, u_first, u_first + n_units),
        grid=(m // tm,),
        in_specs=[row, _resident((1, D_MODEL)),
                  _layer_slab((D_MODEL, D_FF), w_layer), _layer_slab((D_FF, D_MODEL), w_layer), *a_specs],
        out_specs=[row, pl.BlockSpec(a_shape.shape, lambda i: (0,) * len(a_shape.shape))],
        out_shape=[jax.ShapeDtypeStruct((m, D_MODEL), F32), a_shape],
        scratch_shapes=a_scratch,
        compiler_params=_cparams(1, 58),
        name="mlp_attn",
    )(x, g, w_in, w_out, *a_args)
    return xo, ao.reshape(-1, D_MODEL)


def _qkv_attn_kernel(keep_rows, layer, u_first, u_end, unit_offsets, n_aliased,
                     x_ref, g_ref, w_ref, q_ref, t0_ref, t1_ref, t2_ref, tn_ref, c0_hbm, c1_hbm, c2_hbm, *refs):
    qkv_ref, kv0_ref, kv1_ref, kv2_ref, ao_ref, buf0, buf1, buf2, sems = refs[n_aliased:]
    gi, i = pl.program_id(0), pl.program_id(1)
    c_hbms, bufs = (c0_hbm, c1_hbm, c2_hbm), (buf0, buf1, buf2)

    @pl.when((gi == 0) & (i == 0))
    def _():
        for cp in _slab_copies(layer, u_first, c_hbms, bufs, sems):
            cp.start()

    h = _rms(x_ref[...], g_ref[...]).astype(BF16)
    tm = x_ref.shape[0]
    res = []
    for c in range(3):
        r = _dot(h, w_ref[:, c * D_MODEL:(c + 1) * D_MODEL])
        if c == 0:
            r = r * SCALE
        qkv_ref[:, c * D_MODEL:(c + 1) * D_MODEL] = r.astype(qkv_ref.dtype)
        res.append(r)
    for g, kv_ref in enumerate((kv0_ref, kv1_ref, kv2_ref)):
        @pl.when(gi == g)
        def _(g=g, kv_ref=kv_ref):
            kv_ref[:, 0:D_MODEL] = res[1][tm - keep_rows[g]:, :]
            kv_ref[:, D_MODEL:2 * D_MODEL] = res[2][tm - keep_rows[g]:, :]

    @pl.when(i % N_GROUPS == gi)
    def _():
        offset = jnp.where(gi == 0, unit_offsets[0], jnp.where(gi == 1, unit_offsets[1], unit_offsets[2]))
        _carry_unit(layer, u_first, u_end, q_ref, (t0_ref, t1_ref, t2_ref), tn_ref, c_hbms, ao_ref, bufs, sems,
                    u_first + offset + i // N_GROUPS)


def _qkv_attn(x, g, w, layer, batch, seq, kv_bufs, qkv_s, caches_t, tabs, tab_new, u_first):
    m = x.shape[0]
    tm = QKV_TM
    tiles = m // tm
    tpb = seq // tm
    counts = [len(range(p, tiles, N_GROUPS)) for p in range(N_GROUPS)]
    unit_offsets = (0, counts[0], counts[0] + counts[1])
    a_args, a_specs, a_shape, a_scratch = _attn_operands(qkv_s, caches_t, tabs, tab_new, u_first, tiles)
    in_specs = [pl.BlockSpec((tm, D_MODEL), lambda gi, i: (i, 0)),
                _resident((1, D_MODEL)),
                pl.BlockSpec((None, D_MODEL, 3 * D_MODEL), lambda gi, i: (layer, 0, gi),
                             pipeline_mode=pl.Buffered(1)),
                *a_specs, *[pl.BlockSpec(memory_space=pl.ANY)] * len(kv_bufs)]
    out_specs = [pl.BlockSpec((tm, 3 * D_MODEL), lambda gi, i: (i, gi))]
    out_shape = [jax.ShapeDtypeStruct((m, QKV_COLS), BF16)]
    keep_rows = ()
    for g_out, win in enumerate(WINDOWS):
        keep = min(win, seq)
        rows = min(keep, tm)
        n_blk = keep // rows
        keep_rows += (rows,)

        def kv_index(gi, i, g_out=g_out, first=tpb - n_blk, last=n_blk - 1):
            b = jnp.where(gi < g_out, 0, jnp.where(gi > g_out, batch - 1, i // tpb))
            blk = jnp.where(gi < g_out, 0, jnp.where(gi > g_out, last, jnp.maximum(i % tpb - first, 0)))
            return (layer, b, blk, 0)

        out_specs.append(pl.BlockSpec((None, None, rows, 2 * D_MODEL), kv_index))
        out_shape.append(jax.ShapeDtypeStruct((DEPTH // 2, batch, keep, 2 * D_MODEL), F32))
    out_specs.append(pl.BlockSpec(a_shape.shape, lambda gi, i: (0,) * len(a_shape.shape)))
    out_shape.append(a_shape)
    n_in = 3 + len(a_args)
    qkv, kv0, kv1, kv2, ao = pl.pallas_call(
        functools.partial(_qkv_attn_kernel, keep_rows, layer, u_first, u_first + tiles, unit_offsets,
                          len(kv_bufs)),
        grid=(N_GROUPS, tiles),
        in_specs=in_specs, out_specs=out_specs, out_shape=out_shape,
        scratch_shapes=a_scratch,
        input_output_aliases={n_in + k: 1 + k for k in range(len(kv_bufs))},
        compiler_params=_cparams(2, 56),
        name="qkv_attn",
    )(x, g, w, *a_args, *kv_bufs)
    return qkv, (kv0, kv1, kv2), ao.reshape(-1, D_MODEL)


def _proj_kernel(o_ref, x_ref, w_ref, out_ref):
    out_ref[...] = x_ref[...] + _dot(o_ref[...].astype(BF16), w_ref[...])


def _proj_residual(o, x, w, layer):
    m = x.shape[0]
    tm = _row_tile(m, 1024)
    row = pl.BlockSpec((tm, D_MODEL), lambda i: (i, 0))
    return pl.pallas_call(
        _proj_kernel,
        grid=(m // tm,),
        in_specs=[row, row, _layer_slab((D_MODEL, D_MODEL), layer)],
        out_specs=row,
        out_shape=jax.ShapeDtypeStruct((m, D_MODEL), F32),
        compiler_params=_cparams(1, 40),
        name="proj_residual",
    )(o, x, w)


def _norm_kernel(x_ref, g_ref, o_ref):
    o_ref[...] = _rms(x_ref[...], g_ref[...])


def _final_norm(x, g):
    m = x.shape[0]
    tm = _row_tile(m, 1024)
    return pl.pallas_call(
        _norm_kernel,
        grid=(m // tm,),
        in_specs=[pl.BlockSpec((tm, D_MODEL), lambda i: (i, 0)), _resident((1, D_MODEL))],
        out_specs=pl.BlockSpec((tm, D_MODEL), lambda i: (i, 0)),
        out_shape=jax.ShapeDtypeStruct((m, D_MODEL), F32),
        compiler_params=_cparams(1, 32),
        name="final_norm",
    )(x, g)


def _t5_bucket(dist):
    max_exact = NUM_BUCKETS // 2
    n = jnp.maximum(dist.astype(F32), 1.0)
    large = max_exact + (jnp.log(n / max_exact) / math.log(MAX_DISTANCE / max_exact)
                         * (NUM_BUCKETS - max_exact)).astype(jnp.int32)
    large = jnp.minimum(large, NUM_BUCKETS - 1)
    return jnp.where(dist < max_exact, dist, large)


def _toeplitz(w):
    n = 2 * BLK
    tiled = jnp.tile(w, (1,) * (w.ndim - 1) + (BLK,))[..., :BLK * (n - 1)]
    return tiled.reshape(w.shape[:-1] + (BLK, n - 1))[..., :BLK]


def _bias_tables(rel_bias):
    neg_run = jnp.full((N_HEADS, BLK - 1), NEG, F32)
    neg_one = jnp.full((N_HEADS, 1), NEG, F32)
    prompt, tab, tab_new = [], [], []
    for g in range(N_GROUPS):
        d = DILATIONS[g]
        dist = jnp.arange(BLK + 1, dtype=jnp.int32) * d
        tbl = rel_bias[:, g * N_HEADS:(g + 1) * N_HEADS].astype(F32)
        bias = tbl[_t5_bucket(dist)].T
        rev = bias[:, ::-1]
        w_cur = jnp.concatenate([bias[:, 0:1], neg_run, neg_one, rev[:, 1:BLK]], axis=1)
        w_prev = jnp.concatenate([rev[:, 0:BLK], neg_one, neg_run], axis=1)
        t = jnp.concatenate([_toeplitz(w_prev), _toeplitz(w_cur)], axis=-1)
        prompt.append(t.reshape(N_PAIRS, 2 * BLK, 2 * BLK))
        on_grid = rev[:, 0:BLK, None]
        off_grid = jnp.full((N_HEADS, BLK, d - 1), NEG, F32)
        tab.append(jnp.concatenate([on_grid, off_grid], axis=2).reshape(N_HEADS, BLK * d))
        tab_new.append(jnp.broadcast_to(bias[:, 0:1], (N_HEADS, LANES)))
    return jnp.stack(prompt), tab, jnp.stack(tab_new)


def kernel(x_prompt, x_sample, state_conv, cache_kv_w128, cache_kv_w512, cache_kv_w2048,
           norm_mix, norm_mlp, norm_final,
           conv_w_pw1, conv_b_pw1, conv_w_dw, conv_b_dw, conv_ln_g, conv_ln_b, conv_w_pw2, conv_b_pw2,
           attn_w_qkv, attn_w_o, rel_bias, mlp_w_in, mlp_w_out):
    batch, seq, _ = x_prompt.shape
    db = x_sample.shape[0]
    assert x_sample.shape[1] == 1 and seq % (DILATIONS[-1] * BLK) == 0
    caches = (cache_kv_w128, cache_kv_w512, cache_kv_w2048)
    assert all(c.shape[2] == w for c, w in zip(caches, WINDOWS))

    xp = x_prompt.reshape(batch * seq, D_MODEL)
    xs = x_sample.reshape(db, D_MODEL)
    row = lambda v: v.reshape(1, -1)

    tabs_p, tabs_s, tab_new = _bias_tables(rel_bias)
    caches_t = [jnp.transpose(c, (0, 1, 3, 4, 5, 2)) for c in caches]
    state_t = jnp.transpose(state_conv, (0, 2, 1, 3))

    w_pw1, w_pw2 = conv_w_pw1.astype(BF16), conv_w_pw2.astype(BF16)
    w_qkv, w_o = attn_w_qkv.astype(BF16), attn_w_o.astype(BF16)
    w_in, w_out = mlp_w_in.astype(BF16), mlp_w_out.astype(BF16)

    conv_p, conv_s = [], []
    kv_bufs = ()
    kv_s = [[] for _ in range(N_GROUPS)]
    m_rows = batch * seq
    u_mlp, u_qkv = (m_rows // _row_tile(m_rows)) * MLP_UNITS, m_rows // QKV_TM
    assert 2 * u_mlp + u_qkv == db * ATT_NHB
    for j in range(DEPTH // 2):
        ic, ia = 2 * j, 2 * j + 1
        g_mix = row(norm_mix[ic])
        prm = (j, conv_w_dw[j], row(conv_b_dw[j]), row(conv_ln_g[j]), row(conv_ln_b[j]),
               w_pw2, row(conv_b_pw2[j]))
        a_p = _pw1_glu(xp, g_mix, w_pw1, row(conv_b_pw1[j]), j)
        a_s = _pw1_glu(xs, g_mix, w_pw1, row(conv_b_pw1[j]), j)
        xp = _conv_prompt(a_p, xp, batch, seq, *prm)
        xs, ns = _conv_sample(a_s, state_t, xs, *prm)
        conv_p.append(a_p.reshape(batch, seq, D_MODEL)[:, seq - HIST:])
        conv_s.append(jnp.transpose(ns, (1, 0, 2)))
        xs = _mlp(xs, row(norm_mlp[ic]), w_in, w_out, ic)
        g_mix = row(norm_mix[ia])
        qkv_s = _qkv_proj(xs, g_mix, w_qkv, j)
        for g in range(N_GROUPS):
            kv_new = qkv_s[:, (g * 3 + 1) * D_MODEL:(g * 3 + 3) * D_MODEL]
            kv_s[g].append(kv_new.reshape(db, 1, 2, N_HEADS, HEAD_DIM))
        att = (qkv_s, caches_t, j, tabs_s, tab_new)
        xp, o_a = _mlp_attn(xp, row(norm_mlp[ic]), w_in, w_out, ic, *att, 0)
        qkv_p, kv_bufs, o_q = _qkv_attn(xp, g_mix, w_qkv, j, batch, seq, kv_bufs, qkv_s, caches_t, tabs_s,
                                        tab_new, u_mlp)
        xp = _proj_residual(_attn_prompt(qkv_p, tabs_p, batch, seq), xp, w_o, j)
        xp, o_b = _mlp_attn(xp, row(norm_mlp[ia]), w_in, w_out, ia, *att, u_mlp + u_qkv)
        xs = _proj_residual(jnp.concatenate([o_a, o_q, o_b], axis=0), xs, w_o, j)
        xs = _mlp(xs, row(norm_mlp[ia]), w_in, w_out, ia)

    y_prompt = _final_norm(xp, row(norm_final)).reshape(batch, seq, D_MODEL)
    y_sample = _final_norm(xs, row(norm_final)).reshape(db, 1, D_MODEL)
    kv_p = [b.reshape(b.shape[:3] + (2, N_HEADS, HEAD_DIM)) for b in kv_bufs]
    return (y_prompt, y_sample, jnp.stack(conv_p), jnp.stack(conv_s),
            kv_p[0], jnp.stack(kv_s[0]), kv_p[1], jnp.stack(kv_s[1]), kv_p[2], jnp.stack(kv_s[2]))
```

```python
import functools
import math

import jax
import jax.numpy as jnp
from jax import lax
from jax.experimental import pallas as pl
from jax.experimental.pallas import tpu as pltpu

F32 = jnp.float32
BF16 = jnp.bfloat16

D_MODEL = 1024
DEPTH = 4
HEAD_DIM = 64
N_HEADS = D_MODEL // HEAD_DIM
WINDOWS = (128, 512, 2048)
DILATIONS = (1, 4, 16)
N_GROUPS = 3
BLK = 128
SCALE = HEAD_DIM ** -0.5
NUM_BUCKETS = 32
MAX_DISTANCE = 2048
CONV_WIDTH = 31
HIST = CONV_WIDTH - 1
D_FF = 4 * D_MODEL
EPS = 1e-6
NEG = -1e30

LANES = 128
SUBLANES = 8
N_PAIRS = D_MODEL // LANES
HALO = 32
MIB = 1 << 20

assert all(w // d == BLK for w, d in zip(WINDOWS, DILATIONS))
assert 2 * HEAD_DIM == LANES


def _resident(shape):
    nd = len(shape)
    return pl.BlockSpec(shape, lambda *_: (0,) * nd, pipeline_mode=pl.Buffered(1))


def _layer_slab(shape, layer):
    nd = len(shape)
    return pl.BlockSpec((None,) + tuple(shape), lambda *_: (layer,) + (0,) * nd, pipeline_mode=pl.Buffered(1))


def _cparams(n_axes, vmem_mib):
    return pltpu.CompilerParams(dimension_semantics=("arbitrary",) * n_axes,
                                vmem_limit_bytes=vmem_mib * MIB)


def _dot(a, b):
    return jnp.dot(a, b, preferred_element_type=F32)


def _dot_nt(a, b):
    return lax.dot_general(a, b, (((1,), (1,)), ((), ())), preferred_element_type=F32)


def _rms(x, g):
    return x * lax.rsqrt(jnp.mean(x * x, axis=-1, keepdims=True) + EPS) * g


def _sigmoid(x):
    return 1.0 / (1.0 + jnp.exp(-x))


def _row_tile(m, tile=512):
    return tile if m % tile == 0 else m


def _pw1_kernel(x_ref, g_ref, w_ref, b_ref, a_ref):
    h = _rms(x_ref[...], g_ref[...]).astype(BF16)
    u = _dot(h, w_ref[...]) + b_ref[...]
    a_ref[...] = u[:, :D_MODEL] * _sigmoid(u[:, D_MODEL:])


def _pw1_glu(x, g, w, b, layer):
    m = x.shape[0]
    tm = _row_tile(m)
    return pl.pallas_call(
        _pw1_kernel,
        grid=(m // tm,),
        in_specs=[pl.BlockSpec((tm, D_MODEL), lambda i: (i, 0)),
                  _resident((1, D_MODEL)),
                  _layer_slab((D_MODEL, 2 * D_MODEL), layer),
                  _resident((1, 2 * D_MODEL))],
        out_specs=pl.BlockSpec((tm, D_MODEL), lambda i: (i, 0)),
        out_shape=jax.ShapeDtypeStruct((m, D_MODEL), F32),
        compiler_params=_cparams(1, 40),
        name="pw1_glu",
    )(x, g, w, b)


CONV_TS = 256
CONV_RC = 64
CONV_CC = 256


def _ln_swish_pw2(c, lng, lnb, w2, b2):
    mu = jnp.mean(c, axis=-1, keepdims=True)
    cc = c - mu
    var = jnp.mean(cc * cc, axis=-1, keepdims=True)
    z = cc * lax.rsqrt(var + EPS) * lng + lnb
    z = z * _sigmoid(z)
    return _dot(z.astype(BF16), w2) + b2


def _conv_prompt_kernel(a_ref, halo_ref, x_ref, wdw_ref, bdw_ref, lng_ref, lnb_ref,
                        w2_ref, b2_ref, o_ref, full_ref, c_ref):
    i = pl.program_id(1)
    full_ref[0:HALO, :] = jnp.where(i > 0, halo_ref[...], 0.0)
    full_ref[HALO:, :] = a_ref[...]
    off = HALO - HIST
    for r0 in range(0, CONV_TS, CONV_RC):
        for c0 in range(0, D_MODEL, CONV_CC):
            cols = slice(c0, c0 + CONV_CC)
            acc = jnp.broadcast_to(bdw_ref[:, cols], (CONV_RC, CONV_CC))
            for phase in range(SUBLANES):
                part = None
                n_rows = CONV_RC + (SUBLANES if phase else 0)
                for k in range(CONV_WIDTH):
                    if (off + k) % SUBLANES != phase:
                        continue
                    base = r0 + off + k - phase
                    term = wdw_ref[k:k + 1, cols] * full_ref[base:base + n_rows, cols]
                    part = term if part is None else part + term
                acc = acc + part[phase:phase + CONV_RC]
            c_ref[r0:r0 + CONV_RC, cols] = acc
    y = _ln_swish_pw2(c_ref[...], lng_ref[...], lnb_ref[...], w2_ref[...], b2_ref[...])
    o_ref[...] = x_ref[...] + y


def _conv_prompt(a, x, batch, seq, layer, wdw, bdw, lng, lnb, w2, b2):
    m = batch * seq
    nt = seq // CONV_TS
    hpt = CONV_TS // HALO
    row = lambda b, i: (b * nt + i, 0)
    return pl.pallas_call(
        _conv_prompt_kernel,
        grid=(batch, nt),
        in_specs=[pl.BlockSpec((CONV_TS, D_MODEL), row),
                  pl.BlockSpec((HALO, D_MODEL),
                               lambda b, i: (jnp.maximum((b * nt + i) * hpt - 1, 0), 0)),
                  pl.BlockSpec((CONV_TS, D_MODEL), row),
                  _resident((CONV_WIDTH, D_MODEL)),
                  _resident((1, D_MODEL)), _resident((1, D_MODEL)), _resident((1, D_MODEL)),
                  _layer_slab((D_MODEL, D_MODEL), layer), _resident((1, D_MODEL))],
        out_specs=pl.BlockSpec((CONV_TS, D_MODEL), row),
        out_shape=jax.ShapeDtypeStruct((m, D_MODEL), F32),
        scratch_shapes=[pltpu.VMEM((CONV_TS + HALO, D_MODEL), F32),
                        pltpu.VMEM((CONV_TS, D_MODEL), F32)],
        compiler_params=_cparams(2, 32),
        name="conv_prompt",
    )(a, a, x, wdw, bdw, lng, lnb, w2, b2)


CONV_SB = 32


def _conv_sample_kernel(a_ref, st_ref, x_ref, wdw_ref, bdw_ref, lng_ref, lnb_ref,
                        w2_ref, b2_ref, o_ref, ns_ref):
    a = a_ref[...]
    c = a * wdw_ref[HIST:CONV_WIDTH, :] + bdw_ref[...]
    for t in range(HIST):
        st = st_ref[t]
        c = c + st * wdw_ref[t:t + 1, :]
        if t > 0:
            ns_ref[t - 1] = st
    ns_ref[HIST - 1] = a
    y = _ln_swish_pw2(c, lng_ref[...], lnb_ref[...], w2_ref[...], b2_ref[...])
    o_ref[...] = x_ref[...] + y


def _conv_sample(a, state_t, x, layer, wdw, bdw, lng, lnb, w2, b2):
    db = x.shape[0]
    row = lambda i: (i, 0)
    return pl.pallas_call(
        _conv_sample_kernel,
        grid=(db // CONV_SB,),
        in_specs=[pl.BlockSpec((CONV_SB, D_MODEL), row),
                  pl.BlockSpec((None, HIST, CONV_SB, D_MODEL), lambda i: (layer, 0, i, 0)),
                  pl.BlockSpec((CONV_SB, D_MODEL), row),
                  _resident((CONV_WIDTH, D_MODEL)),
                  _resident((1, D_MODEL)), _resident((1, D_MODEL)), _resident((1, D_MODEL)),
                  _layer_slab((D_MODEL, D_MODEL), layer), _resident((1, D_MODEL))],
        out_specs=[pl.BlockSpec((CONV_SB, D_MODEL), row),
                   pl.BlockSpec((HIST, CONV_SB, D_MODEL), lambda i: (0, i, 0))],
        out_shape=[jax.ShapeDtypeStruct((db, D_MODEL), F32),
                   jax.ShapeDtypeStruct((HIST, db, D_MODEL), F32)],
        compiler_params=_cparams(1, 40),
        name="conv_sample",
    )(a, state_t, x, wdw, bdw, lng, lnb, w2, b2)


MLP_FC = 1024


def _mlp_kernel(x_ref, g_ref, win_ref, wout_ref, o_ref):
    x = x_ref[...]
    h = _rms(x, g_ref[...]).astype(BF16)
    acc = x
    for c0 in range(0, D_FF, MLP_FC):
        u = jnp.maximum(_dot(h, win_ref[:, c0:c0 + MLP_FC]), 0.0)
        acc = acc + _dot((u * u).astype(BF16), wout_ref[c0:c0 + MLP_FC, :])
    o_ref[...] = acc


def _mlp(x, g, w_in, w_out, layer):
    m = x.shape[0]
    tm = _row_tile(m)
    return pl.pallas_call(
        _mlp_kernel,
        grid=(m // tm,),
        in_specs=[pl.BlockSpec((tm, D_MODEL), lambda i: (i, 0)),
                  _resident((1, D_MODEL)),
                  _layer_slab((D_MODEL, D_FF), layer),
                  _layer_slab((D_FF, D_MODEL), layer)],
        out_specs=pl.BlockSpec((tm, D_MODEL), lambda i: (i, 0)),
        out_shape=jax.ShapeDtypeStruct((m, D_MODEL), F32),
        compiler_params=_cparams(1, 48),
        name="mlp",
    )(x, g, w_in, w_out)


QKV_TM = 256
QKV_COLS = N_GROUPS * 3 * D_MODEL


def _qkv_kernel(keep_rows, n_aliased, x_ref, g_ref, w_ref, *refs):
    qkv_ref, *kv_refs = refs[n_aliased:]
    h = _rms(x_ref[...], g_ref[...]).astype(BF16)
    tm = x_ref.shape[0]
    for g in range(N_GROUPS):
        for c in range(3):
            col = (g * 3 + c) * D_MODEL
            r = _dot(h, w_ref[:, col:col + D_MODEL])
            if c == 0:
                r = r * SCALE
            qkv_ref[:, col:col + D_MODEL] = r.astype(qkv_ref.dtype)
            if kv_refs and c > 0:
                kv_refs[g][:, (c - 1) * D_MODEL:c * D_MODEL] = r[tm - keep_rows[g]:, :]


def _qkv_proj(x, g, w, layer, *, batch=None, seq=None, kv_bufs=()):
    m = x.shape[0]
    prompt = batch is not None
    tm = QKV_TM if prompt else m
    in_specs = [pl.BlockSpec((tm, D_MODEL), lambda i: (i, 0)),
                _resident((1, D_MODEL)),
                _layer_slab((D_MODEL, QKV_COLS), layer)]
    out_specs = [pl.BlockSpec((tm, QKV_COLS), lambda i: (i, 0))]
    out_shape = [jax.ShapeDtypeStruct((m, QKV_COLS), BF16 if prompt else F32)]
    keep_rows = ()
    if prompt:
        tpb = seq // tm
        for win in WINDOWS:
            keep = min(win, seq)
            rows = min(keep, tm)
            first = tpb - keep // rows
            keep_rows += (rows,)
            out_specs.append(pl.BlockSpec(
                (None, None, rows, 2 * D_MODEL),
                lambda i, first=first: (layer, i // tpb, jnp.maximum(i % tpb - first, 0), 0)))
            out_shape.append(jax.ShapeDtypeStruct((DEPTH // 2, batch, keep, 2 * D_MODEL), F32))
    in_specs += [pl.BlockSpec(memory_space=pl.ANY)] * len(kv_bufs)
    return pl.pallas_call(
        functools.partial(_qkv_kernel, keep_rows, len(kv_bufs)),
        grid=(m // tm,),
        in_specs=in_specs, out_specs=out_specs, out_shape=out_shape,
        input_output_aliases={3 + k: 1 + k for k in range(len(kv_bufs))},
        compiler_params=_cparams(1, 56),
        name="qkv_proj",
    )(x, g, w, *kv_bufs)


MERGE_ROWS = 256


def _attn_prompt_kernel(*refs):
    qkv_refs, (tab_ref, o_ref, f_scr, og_scr, lg_scr) = refs[:9], refs[9:]
    seq = o_ref.shape[0]
    lo_q = lax.broadcasted_iota(jnp.int32, (BLK, LANES), 1) < HEAD_DIM
    m_lo, m_hi = {}, {}
    for n_rows in (BLK, 2 * BLK):
        lo = lax.broadcasted_iota(jnp.int32, (n_rows, LANES), 1) < HEAD_DIM
        m_lo[n_rows] = jnp.where(lo, 1.0, 0.0).astype(BF16)
        m_hi[n_rows] = jnp.where(lo, 0.0, 1.0).astype(BF16)

    for g in range(N_GROUPS):
        d = DILATIONS[g]
        if d == 1:
            src = qkv_refs[0:3]
        else:
            src = []
            for c in range(3):
                f_scr[(g - 1) * 3 + c] = qkv_refs[g * 3 + c][...].astype(F32)
                src.append(f_scr.at[(g - 1) * 3 + c])

        def rows(start, d=d):
            return pl.ds(start, BLK) if d == 1 else pl.ds(start, BLK, stride=d)

        def load(c, start, src=src):
            return src[c][rows(start), :].astype(BF16)

        n_blk = seq // d // BLK
        for r in range(d):
            for n in range(n_blk):
                start = r + n * BLK * d
                q = load(0, start)
                kc, vc = load(1, start), load(2, start)
                if n == 0:
                    keys, vals, tab = kc, vc, tab_ref[g, :, BLK:]
                else:
                    keys = jnp.concatenate([kp, kc], axis=0)
                    vals = jnp.concatenate([vp, vc], axis=0)
                    tab = tab_ref[g]
                kp, vp = kc, vc
                nk = keys.shape[0]
                q2 = jnp.concatenate([q * m_lo[BLK], q * m_hi[BLK]], axis=0)
                s = _dot_nt(q2, keys) + tab
                mx = jnp.max(s, axis=-1, keepdims=True)
                p = jnp.exp(s - mx).astype(BF16)
                pcat = jnp.concatenate([p[:BLK], p[BLK:]], axis=1)
                rhs = jnp.concatenate(
                    [jnp.concatenate([vals * m_lo[nk], m_lo[nk]], axis=1),
                     jnp.concatenate([vals * m_hi[nk], m_hi[nk]], axis=1)], axis=0)
                res = _dot(pcat, rhs)
                den = res[:, LANES:]
                og_scr[g, rows(start), :] = res[:, :LANES] / den
                lg_scr[g, rows(start), :] = jnp.where(lo_q, mx[:BLK], mx[BLK:]) + jnp.log(den)

    for r0 in range(0, seq, MERGE_ROWS):
        sl = slice(r0, r0 + MERGE_ROWS)
        l0, l1, l2 = lg_scr[0, sl, :], lg_scr[1, sl, :], lg_scr[2, sl, :]
        mx = jnp.maximum(jnp.maximum(l0, l1), l2)
        e0, e1, e2 = jnp.exp(l0 - mx), jnp.exp(l1 - mx), jnp.exp(l2 - mx)
        inv = 1.0 / (e0 + e1 + e2)
        o = (e0 * inv) * og_scr[0, sl, :] + (e1 * inv) * og_scr[1, sl, :] + (e2 * inv) * og_scr[2, sl, :]
        o_ref[sl, :] = o.astype(o_ref.dtype)


def _attn_prompt(qkv, tabs, batch, seq):
    m = batch * seq

    def spec(g, c):
        return pl.BlockSpec((seq, LANES), lambda b, p: (b, (g * 3 + c) * N_PAIRS + p))

    return pl.pallas_call(
        _attn_prompt_kernel,
        grid=(batch, N_PAIRS),
        in_specs=[spec(g, c) for g in range(N_GROUPS) for c in range(3)]
        + [pl.BlockSpec((N_GROUPS, None, 2 * BLK, 2 * BLK), lambda b, p: (0, p, 0, 0))],
        out_specs=pl.BlockSpec((seq, LANES), lambda b, p: (b, p)),
        out_shape=jax.ShapeDtypeStruct((m, D_MODEL), BF16),
        scratch_shapes=[pltpu.VMEM((6, seq, LANES), F32),
                        pltpu.VMEM((N_GROUPS, seq, LANES), F32),
                        pltpu.VMEM((N_GROUPS, seq, LANES), F32)],
        compiler_params=_cparams(2, 48),
        name="attn_prompt",
    )(*([qkv] * 9), tabs)


ATT_HB = SUBLANES
ATT_PAIRS = ATT_HB // 2


def _sample_attn_unit(q_refs, c_refs, t_refs, tn_refs):
    lane = lax.broadcasted_iota(jnp.int32, (HEAD_DIM, LANES), 1)
    head_of_lane = 2 * lax.broadcasted_iota(jnp.int32, (ATT_HB, LANES), 1)
    head_row = lax.broadcasted_iota(jnp.int32, (ATT_HB, LANES), 0)
    pad = jnp.zeros((LANES - 3 * ATT_PAIRS, LANES), F32)
    heads = range(ATT_HB)
    stack = lambda per_head: jnp.concatenate(per_head, axis=0)

    unnorm, dens, lses = [], [], []
    for q_ref, c_ref, t_ref, tn_ref in zip(q_refs, c_refs, t_refs, tn_refs):
        qt = jnp.concatenate([q_ref[...], pad], axis=0).T

        def col(c, hl, qt=qt):
            j = c * ATT_PAIRS + hl // 2
            return qt[(hl % 2) * HEAD_DIM:(hl % 2 + 1) * HEAD_DIM, j:j + 1]

        qb = [jnp.broadcast_to(col(0, hl), (HEAD_DIM, LANES)) for hl in heads]
        cks = [slice(k, k + LANES) for k in range(0, c_ref.shape[-1], LANES)]
        s_new = stack([jnp.sum(col(0, hl) * col(1, hl), axis=0, keepdims=True) for hl in heads]) + tn_ref[...]
        scores = [stack([jnp.sum(c_ref[0, hl, :, ck] * qb[hl], axis=0, keepdims=True) for hl in heads])
                  + t_ref[:, ck] for ck in cks]
        mx = jnp.max(functools.reduce(jnp.maximum, scores), axis=-1, keepdims=True)
        mx = jnp.maximum(mx, s_new)
        p = [jnp.exp(sc - mx) for sc in scores]
        pn = jnp.exp(s_new - mx)
        den = jnp.sum(functools.reduce(jnp.add, p), axis=-1, keepdims=True) + pn
        halves = [jnp.zeros((HEAD_DIM, LANES), F32) for _ in range(2)]
        for hl in heads:
            acc = functools.reduce(jnp.add, [c_ref[1, hl, :, ck] * pc[hl:hl + 1, :]
                                             for ck, pc in zip(cks, p)])
            un = jnp.sum(acc, axis=-1, keepdims=True) + pn[hl:hl + 1, 0:1] * col(2, hl)
            halves[hl % 2] = jnp.where(lane == hl // 2, un, halves[hl % 2])
        unnorm.append(halves)
        dens.append(den)
        lses.append(mx + jnp.log(den))

    lmx = jnp.maximum(jnp.maximum(lses[0], lses[1]), lses[2])
    es = [jnp.exp(l - lmx) for l in lses]
    tot = es[0] + es[1] + es[2]
    coef = [e / (tot * dn) for e, dn in zip(es, dens)]
    out = []
    for parity in range(2):
        o = None
        for g in range(N_GROUPS):
            c_row = jnp.sum(jnp.where(head_row == head_of_lane + parity, coef[g], 0.0), axis=0, keepdims=True)
            o = unnorm[g][parity] * c_row if o is None else o + unnorm[g][parity] * c_row
        out.append(o)
    return jnp.concatenate(out, axis=0).T[0:ATT_PAIRS, :]


ATT_NHB = N_HEADS // ATT_HB
MLP_CHUNKS = D_FF // MLP_FC
MLP_UNITS = MLP_CHUNKS
TILE_ROWS = MLP_UNITS // ATT_NHB
assert MLP_UNITS % 2 == 0 and MLP_UNITS % ATT_NHB == 0


def _slab_copies(layer, row, k, c_hbms, bufs, sems):
    heads = pl.ds((k % ATT_NHB) * ATT_HB, ATT_HB)
    return [pltpu.make_async_copy(c_hbm.at[layer, row + k // ATT_NHB, :, heads], buf.at[k % 2], sems.at[g, k % 2])
            for g, (c_hbm, buf) in enumerate(zip(c_hbms, bufs))]


def _attn_operands(qkv_s, caches_t, tabs, tab_new, u_first, n_units):
    db = qkv_s.shape[0]
    assert u_first % ATT_NHB == 0 and n_units % ATT_NHB == 0
    b0, rows = u_first // ATT_NHB, n_units // ATT_NHB
    q = qkv_s[b0:b0 + rows].reshape(rows, N_GROUPS, 3, ATT_NHB, ATT_PAIRS, LANES).transpose(1, 0, 3, 2, 4, 5)
    q = q.reshape(N_GROUPS, rows, ATT_NHB, 3 * ATT_PAIRS, LANES)
    args = [q, *[t.reshape(ATT_NHB, ATT_HB, -1) for t in tabs],
            tab_new.reshape(N_GROUPS, ATT_NHB, ATT_HB, LANES), *caches_t]
    specs = [pl.BlockSpec((N_GROUPS, TILE_ROWS) + q.shape[2:], lambda i: (0, i, 0, 0, 0)),
             *[_resident((ATT_NHB, ATT_HB, w)) for w in WINDOWS],
             _resident((N_GROUPS, ATT_NHB, ATT_HB, LANES)), *[pl.BlockSpec(memory_space=pl.ANY)] * N_GROUPS]
    out_shape = jax.ShapeDtypeStruct((rows, ATT_NHB, ATT_PAIRS, LANES), F32)
    scratch = [*[pltpu.VMEM((2, 2, ATT_HB, HEAD_DIM, w), F32) for w in WINDOWS],
               pltpu.SemaphoreType.DMA((N_GROUPS, 2))]
    return args, specs, out_shape, scratch


def _mlp_attn_kernel(layer, u_first, final_norm, has_proj, x_ref, g_ref, gf_ref, win_ref, wout_ref, *refs):
    if has_proj:
        oin_ref, wo_ref, *refs = refs
    (q_ref, t0_ref, t1_ref, t2_ref, tn_ref, c0_hbm, c1_hbm, c2_hbm, o_ref, ao_ref,
     buf0, buf1, buf2, sems) = refs
    i = pl.program_id(0)
    n_steps = pl.num_programs(0)
    c_hbms, bufs, t_refs = (c0_hbm, c1_hbm, c2_hbm), (buf0, buf1, buf2), (t0_ref, t1_ref, t2_ref)
    copies = functools.partial(_slab_copies, layer, c_hbms=c_hbms, bufs=bufs, sems=sems)
    row = u_first // ATT_NHB + i * TILE_ROWS

    @pl.when(i == 0)
    def _():
        for cp in copies(row, 0):
            cp.start()

    x = x_ref[...]
    if has_proj:
        x = x + _dot(oin_ref[...].astype(BF16), wo_ref[...])
    h = _rms(x, g_ref[...]).astype(BF16)
    acc = x
    for k in range(MLP_CHUNKS):
        if k + 1 < MLP_UNITS:
            for cp in copies(row, k + 1):
                cp.start()
        else:
            @pl.when(i + 1 < n_steps)
            def _():
                for cp in copies(row + TILE_ROWS, 0):
                    cp.start()
        for cp in copies(row, k):
            cp.wait()
        c0 = k * MLP_FC
        u = jnp.maximum(_dot(h, win_ref[:, c0:c0 + MLP_FC]), 0.0)
        acc = acc + _dot((u * u).astype(BF16), wout_ref[c0:c0 + MLP_FC, :])
        bl, hb = k // ATT_NHB, k % ATT_NHB
        ao_ref[bl, hb] = _sample_attn_unit(
            [q_ref.at[g, bl, hb] for g in range(N_GROUPS)],
            [buf.at[k % 2] for buf in bufs],
            [t_ref.at[hb] for t_ref in t_refs],
            [tn_ref.at[g, hb] for g in range(N_GROUPS)])
    o_ref[...] = _rms(acc, gf_ref[...]) if final_norm else acc


def _mlp_attn(x, g, g_final, w_in, w_out, w_layer, qkv_s, caches_t, cache_layer, tabs, tab_new, u_first,
              final_norm, proj=None):
    m = x.shape[0]
    tm = _row_tile(m)
    n_units = (m // tm) * MLP_UNITS
    a_args, a_specs, a_shape, a_scratch = _attn_operands(qkv_s, caches_t, tabs, tab_new, u_first, n_units)
    row = pl.BlockSpec((tm, D_MODEL), lambda i: (i, 0))
    p_args, p_specs = ([], []) if proj is None else (list(proj[:2]), [row, _layer_slab((D_MODEL, D_MODEL), proj[2])])
    xo, ao = pl.pallas_call(
        functools.partial(_mlp_attn_kernel, cache_layer, u_first, final_norm, proj is not None),
        grid=(m // tm,),
        in_specs=[row, _resident((1, D_MODEL)), _resident((1, D_MODEL)),
                  _layer_slab((D_MODEL, D_FF), w_layer), _layer_slab((D_FF, D_MODEL), w_layer),
                  *p_specs, *a_specs],
        out_specs=[row, pl.BlockSpec((TILE_ROWS,) + a_shape.shape[1:], lambda i: (i, 0, 0, 0))],
        out_shape=[jax.ShapeDtypeStruct((m, D_MODEL), F32), a_shape],
        scratch_shapes=a_scratch,
        compiler_params=_cparams(1, 58),
        name="mlp_attn",
    )(x, g, g_final, w_in, w_out, *p_args, *a_args)
    return xo, ao.reshape(-1, D_MODEL)


def _proj_kernel(o_ref, x_ref, w_ref, out_ref):
    out_ref[...] = x_ref[...] + _dot(o_ref[...].astype(BF16), w_ref[...])


def _proj_residual(o, x, w, layer):
    m = x.shape[0]
    tm = _row_tile(m, 1024)
    row = pl.BlockSpec((tm, D_MODEL), lambda i: (i, 0))
    return pl.pallas_call(
        _proj_kernel,
        grid=(m // tm,),
        in_specs=[row, row, _layer_slab((D_MODEL, D_MODEL), layer)],
        out_specs=row,
        out_shape=jax.ShapeDtypeStruct((m, D_MODEL), F32),
        compiler_params=_cparams(1, 40),
        name="proj_residual",
    )(o, x, w)


def _norm_kernel(x_ref, g_ref, o_ref):
    o_ref[...] = _rms(x_ref[...], g_ref[...])


def _final_norm(x, g):
    m = x.shape[0]
    tm = _row_tile(m, 1024)
    return pl.pallas_call(
        _norm_kernel,
        grid=(m // tm,),
        in_specs=[pl.BlockSpec((tm, D_MODEL), lambda i: (i, 0)), _resident((1, D_MODEL))],
        out_specs=pl.BlockSpec((tm, D_MODEL), lambda i: (i, 0)),
        out_shape=jax.ShapeDtypeStruct((m, D_MODEL), F32),
        compiler_params=_cparams(1, 32),
        name="final_norm",
    )(x, g)


def _t5_bucket(dist):
    max_exact = NUM_BUCKETS // 2
    n = jnp.maximum(dist.astype(F32), 1.0)
    large = max_exact + (jnp.log(n / max_exact) / math.log(MAX_DISTANCE / max_exact)
                         * (NUM_BUCKETS - max_exact)).astype(jnp.int32)
    large = jnp.minimum(large, NUM_BUCKETS - 1)
    return jnp.where(dist < max_exact, dist, large)


def _toeplitz(w):
    n = 2 * BLK
    tiled = jnp.tile(w, (1,) * (w.ndim - 1) + (BLK,))[..., :BLK * (n - 1)]
    return tiled.reshape(w.shape[:-1] + (BLK, n - 1))[..., :BLK]


def _bias_tables(rel_bias):
    neg_run = jnp.full((N_HEADS, BLK - 1), NEG, F32)
    neg_one = jnp.full((N_HEADS, 1), NEG, F32)
    prompt, tab, tab_new = [], [], []
    for g in range(N_GROUPS):
        d = DILATIONS[g]
        dist = jnp.arange(BLK + 1, dtype=jnp.int32) * d
        tbl = rel_bias[:, g * N_HEADS:(g + 1) * N_HEADS].astype(F32)
        bias = tbl[_t5_bucket(dist)].T
        rev = bias[:, ::-1]
        w_cur = jnp.concatenate([bias[:, 0:1], neg_run, neg_one, rev[:, 1:BLK]], axis=1)
        w_prev = jnp.concatenate([rev[:, 0:BLK], neg_one, neg_run], axis=1)
        t = jnp.concatenate([_toeplitz(w_prev), _toeplitz(w_cur)], axis=-1)
        prompt.append(t.reshape(N_PAIRS, 2 * BLK, 2 * BLK))
        on_grid = rev[:, 0:BLK, None]
        off_grid = jnp.full((N_HEADS, BLK, d - 1), NEG, F32)
        tab.append(jnp.concatenate([on_grid, off_grid], axis=2).reshape(N_HEADS, BLK * d))
        tab_new.append(jnp.broadcast_to(bias[:, 0:1], (N_HEADS, LANES)))
    return jnp.stack(prompt), tab, jnp.stack(tab_new)


def kernel(x_prompt, x_sample, state_conv, cache_kv_w128, cache_kv_w512, cache_kv_w2048,
           norm_mix, norm_mlp, norm_final,
           conv_w_pw1, conv_b_pw1, conv_w_dw, conv_b_dw, conv_ln_g, conv_ln_b, conv_w_pw2, conv_b_pw2,
           attn_w_qkv, attn_w_o, rel_bias, mlp_w_in, mlp_w_out):
    batch, seq, _ = x_prompt.shape
    db = x_sample.shape[0]
    assert x_sample.shape[1] == 1 and seq % (DILATIONS[-1] * BLK) == 0
    caches = (cache_kv_w128, cache_kv_w512, cache_kv_w2048)
    assert all(c.shape[2] == w for c, w in zip(caches, WINDOWS))

    xp = x_prompt.reshape(batch * seq, D_MODEL)
    xs = x_sample.reshape(db, D_MODEL)
    row = lambda v: v.reshape(1, -1)

    tabs_p, tabs_s, tab_new = _bias_tables(rel_bias)
    caches_t = [jnp.transpose(c, (0, 1, 3, 4, 5, 2)) for c in caches]
    state_t = jnp.transpose(state_conv, (0, 2, 1, 3))

    w_pw1, w_pw2 = conv_w_pw1.astype(BF16), conv_w_pw2.astype(BF16)
    w_qkv, w_o = attn_w_qkv.astype(BF16), attn_w_o.astype(BF16)
    w_in, w_out = mlp_w_in.astype(BF16), mlp_w_out.astype(BF16)

    conv_p, conv_s = [], []
    kv_bufs = ()
    kv_s = [[] for _ in range(N_GROUPS)]
    m_rows = batch * seq
    u_mlp = (m_rows // _row_tile(m_rows)) * MLP_UNITS
    assert 2 * u_mlp == db * ATT_NHB
    g_final = row(norm_final)
    for j in range(DEPTH // 2):
        ic, ia = 2 * j, 2 * j + 1
        last = ia == DEPTH - 1
        g_mix = row(norm_mix[ic])
        prm = (j, conv_w_dw[j], row(conv_b_dw[j]), row(conv_ln_g[j]), row(conv_ln_b[j]),
               w_pw2, row(conv_b_pw2[j]))
        a_p = _pw1_glu(xp, g_mix, w_pw1, row(conv_b_pw1[j]), j)
        a_s = _pw1_glu(xs, g_mix, w_pw1, row(conv_b_pw1[j]), j)
        xp = _conv_prompt(a_p, xp, batch, seq, *prm)
        xs, ns = _conv_sample(a_s, state_t, xs, *prm)
        conv_p.append(a_p.reshape(batch, seq, D_MODEL)[:, seq - HIST:])
        conv_s.append(jnp.transpose(ns, (1, 0, 2)))
        xs = _mlp(xs, row(norm_mlp[ic]), w_in, w_out, ic)
        g_mix = row(norm_mix[ia])
        (qkv_s,) = _qkv_proj(xs, g_mix, w_qkv, j)
        for g in range(N_GROUPS):
            kv_new = qkv_s[:, (g * 3 + 1) * D_MODEL:(g * 3 + 3) * D_MODEL]
            kv_s[g].append(kv_new.reshape(db, 1, 2, N_HEADS, HEAD_DIM))
        att = (qkv_s, caches_t, j, tabs_s, tab_new)
        xp, o_a = _mlp_attn(xp, row(norm_mlp[ic]), g_final, w_in, w_out, ic, *att, 0, False)
        qkv_p, *kv_bufs = _qkv_proj(xp, g_mix, w_qkv, j, batch=batch, seq=seq, kv_bufs=kv_bufs)
        o_p = _attn_prompt(qkv_p, tabs_p, batch, seq)
        xp, o_b = _mlp_attn(xp, row(norm_mlp[ia]), g_final, w_in, w_out, ia, *att, u_mlp, last,
                            proj=(o_p, w_o, j))
        xs = _proj_residual(jnp.concatenate([o_a, o_b], axis=0), xs, w_o, j)
        xs = _mlp(xs, row(norm_mlp[ia]), w_in, w_out, ia)

    y_prompt = xp.reshape(batch, seq, D_MODEL)
    y_sample = _final_norm(xs, g_final).reshape(db, 1, D_MODEL)
    kv_p = [b.reshape(b.shape[:3] + (2, N_HEADS, HEAD_DIM)) for b in kv_bufs]
    return (y_prompt, y_sample, jnp.stack(conv_p), jnp.stack(conv_s),
            kv_p[0], jnp.stack(kv_s[0]), kv_p[1], jnp.stack(kv_s[1]), kv_p[2], jnp.stack(kv_s[2]))
```

```python
import functools
import math

import jax
import jax.numpy as jnp
from jax import lax
from jax.experimental import pallas as pl
from jax.experimental.pallas import tpu as pltpu

F32 = jnp.float32
BF16 = jnp.bfloat16

D_MODEL = 1024
DEPTH = 4
HEAD_DIM = 64
N_HEADS = D_MODEL // HEAD_DIM
WINDOWS = (128, 512, 2048)
DILATIONS = (1, 4, 16)
N_GROUPS = 3
BLK = 128
SCALE = HEAD_DIM ** -0.5
NUM_BUCKETS = 32
MAX_DISTANCE = 2048
CONV_WIDTH = 31
HIST = CONV_WIDTH - 1
D_FF = 4 * D_MODEL
EPS = 1e-6
NEG = -1e30

LANES = 128
SUBLANES = 8
N_PAIRS = D_MODEL // LANES
HALO = 32
MIB = 1 << 20

assert all(w // d == BLK for w, d in zip(WINDOWS, DILATIONS))
assert 2 * HEAD_DIM == LANES


def _resident(shape):
    nd = len(shape)
    return pl.BlockSpec(shape, lambda *_: (0,) * nd, pipeline_mode=pl.Buffered(1))


def _layer_slab(shape, layer):
    nd = len(shape)
    return pl.BlockSpec((None,) + tuple(shape), lambda *_: (layer,) + (0,) * nd, pipeline_mode=pl.Buffered(1))


def _cparams(n_axes, vmem_mib):
    return pltpu.CompilerParams(dimension_semantics=("arbitrary",) * n_axes,
                                vmem_limit_bytes=vmem_mib * MIB)


def _dot(a, b):
    return jnp.dot(a, b, preferred_element_type=F32)


def _dot_nt(a, b):
    return lax.dot_general(a, b, (((1,), (1,)), ((), ())), preferred_element_type=F32)


def _rms(x, g):
    return x * lax.rsqrt(jnp.mean(x * x, axis=-1, keepdims=True) + EPS) * g


def _sigmoid(x):
    return 1.0 / (1.0 + jnp.exp(-x))


def _row_tile(m, tile=512):
    return tile if m % tile == 0 else m


def _pw1_kernel(x_ref, g_ref, w_ref, b_ref, a_ref):
    h = _rms(x_ref[...], g_ref[...]).astype(BF16)
    u = _dot(h, w_ref[...]) + b_ref[...]
    a_ref[...] = u[:, :D_MODEL] * _sigmoid(u[:, D_MODEL:])


def _pw1_glu(x, g, w, b, layer):
    m = x.shape[0]
    tm = _row_tile(m)
    return pl.pallas_call(
        _pw1_kernel,
        grid=(m // tm,),
        in_specs=[pl.BlockSpec((tm, D_MODEL), lambda i: (i, 0)),
                  _resident((1, D_MODEL)),
                  _layer_slab((D_MODEL, 2 * D_MODEL), layer),
                  _resident((1, 2 * D_MODEL))],
        out_specs=pl.BlockSpec((tm, D_MODEL), lambda i: (i, 0)),
        out_shape=jax.ShapeDtypeStruct((m, D_MODEL), F32),
        compiler_params=_cparams(1, 40),
        name="pw1_glu",
    )(x, g, w, b)


CONV_TS = 256
CONV_RC = 64
CONV_CC = 256


def _ln_swish_pw2(c, lng, lnb, w2, b2):
    mu = jnp.mean(c, axis=-1, keepdims=True)
    cc = c - mu
    var = jnp.mean(cc * cc, axis=-1, keepdims=True)
    z = cc * lax.rsqrt(var + EPS) * lng + lnb
    z = z * _sigmoid(z)
    return _dot(z.astype(BF16), w2) + b2


def _conv_prompt_kernel(a_ref, halo_ref, x_ref, wdw_ref, bdw_ref, lng_ref, lnb_ref,
                        w2_ref, b2_ref, o_ref, full_ref, c_ref):
    i = pl.program_id(1)
    full_ref[0:HALO, :] = jnp.where(i > 0, halo_ref[...], 0.0)
    full_ref[HALO:, :] = a_ref[...]
    off = HALO - HIST
    for r0 in range(0, CONV_TS, CONV_RC):
        for c0 in range(0, D_MODEL, CONV_CC):
            cols = slice(c0, c0 + CONV_CC)
            acc = jnp.broadcast_to(bdw_ref[:, cols], (CONV_RC, CONV_CC))
            for phase in range(SUBLANES):
                part = None
                n_rows = CONV_RC + (SUBLANES if phase else 0)
                for k in range(CONV_WIDTH):
                    if (off + k) % SUBLANES != phase:
                        continue
                    base = r0 + off + k - phase
                    term = wdw_ref[k:k + 1, cols] * full_ref[base:base + n_rows, cols]
                    part = term if part is None else part + term
                acc = acc + part[phase:phase + CONV_RC]
            c_ref[r0:r0 + CONV_RC, cols] = acc
    y = _ln_swish_pw2(c_ref[...], lng_ref[...], lnb_ref[...], w2_ref[...], b2_ref[...])
    o_ref[...] = x_ref[...] + y


def _conv_prompt(a, x, batch, seq, layer, wdw, bdw, lng, lnb, w2, b2):
    m = batch * seq
    nt = seq // CONV_TS
    hpt = CONV_TS // HALO
    row = lambda b, i: (b * nt + i, 0)
    return pl.pallas_call(
        _conv_prompt_kernel,
        grid=(batch, nt),
        in_specs=[pl.BlockSpec((CONV_TS, D_MODEL), row),
                  pl.BlockSpec((HALO, D_MODEL),
                               lambda b, i: (jnp.maximum((b * nt + i) * hpt - 1, 0), 0)),
                  pl.BlockSpec((CONV_TS, D_MODEL), row),
                  _resident((CONV_WIDTH, D_MODEL)),
                  _resident((1, D_MODEL)), _resident((1, D_MODEL)), _resident((1, D_MODEL)),
                  _layer_slab((D_MODEL, D_MODEL), layer), _resident((1, D_MODEL))],
        out_specs=pl.BlockSpec((CONV_TS, D_MODEL), row),
        out_shape=jax.ShapeDtypeStruct((m, D_MODEL), F32),
        scratch_shapes=[pltpu.VMEM((CONV_TS + HALO, D_MODEL), F32),
                        pltpu.VMEM((CONV_TS, D_MODEL), F32)],
        compiler_params=_cparams(2, 32),
        name="conv_prompt",
    )(a, a, x, wdw, bdw, lng, lnb, w2, b2)


CONV_SB = 32


def _conv_sample_kernel(a_ref, st_ref, x_ref, wdw_ref, bdw_ref, lng_ref, lnb_ref,
                        w2_ref, b2_ref, o_ref, ns_ref):
    a = a_ref[...]
    c = a * wdw_ref[HIST:CONV_WIDTH, :] + bdw_ref[...]
    for t in range(HIST):
        st = st_ref[t]
        c = c + st * wdw_ref[t:t + 1, :]
        if t > 0:
            ns_ref[t - 1] = st
    ns_ref[HIST - 1] = a
    y = _ln_swish_pw2(c, lng_ref[...], lnb_ref[...], w2_ref[...], b2_ref[...])
    o_ref[...] = x_ref[...] + y


def _conv_sample(a, state_t, x, layer, wdw, bdw, lng, lnb, w2, b2):
    db = x.shape[0]
    row = lambda i: (i, 0)
    return pl.pallas_call(
        _conv_sample_kernel,
        grid=(db // CONV_SB,),
        in_specs=[pl.BlockSpec((CONV_SB, D_MODEL), row),
                  pl.BlockSpec((None, HIST, CONV_SB, D_MODEL), lambda i: (layer, 0, i, 0)),
                  pl.BlockSpec((CONV_SB, D_MODEL), row),
                  _resident((CONV_WIDTH, D_MODEL)),
                  _resident((1, D_MODEL)), _resident((1, D_MODEL)), _resident((1, D_MODEL)),
                  _layer_slab((D_MODEL, D_MODEL), layer), _resident((1, D_MODEL))],
        out_specs=[pl.BlockSpec((CONV_SB, D_MODEL), row),
                   pl.BlockSpec((HIST, CONV_SB, D_MODEL), lambda i: (0, i, 0))],
        out_shape=[jax.ShapeDtypeStruct((db, D_MODEL), F32),
                   jax.ShapeDtypeStruct((HIST, db, D_MODEL), F32)],
        compiler_params=_cparams(1, 40),
        name="conv_sample",
    )(a, state_t, x, wdw, bdw, lng, lnb, w2, b2)


MLP_FC = 1024


def _mlp_kernel(x_ref, g_ref, win_ref, wout_ref, o_ref):
    x = x_ref[...]
    h = _rms(x, g_ref[...]).astype(BF16)
    acc = x
    for c0 in range(0, D_FF, MLP_FC):
        u = jnp.maximum(_dot(h, win_ref[:, c0:c0 + MLP_FC]), 0.0)
        acc = acc + _dot((u * u).astype(BF16), wout_ref[c0:c0 + MLP_FC, :])
    o_ref[...] = acc


def _mlp(x, g, w_in, w_out, layer):
    m = x.shape[0]
    tm = _row_tile(m)
    return pl.pallas_call(
        _mlp_kernel,
        grid=(m // tm,),
        in_specs=[pl.BlockSpec((tm, D_MODEL), lambda i: (i, 0)),
                  _resident((1, D_MODEL)),
                  _layer_slab((D_MODEL, D_FF), layer),
                  _layer_slab((D_FF, D_MODEL), layer)],
        out_specs=pl.BlockSpec((tm, D_MODEL), lambda i: (i, 0)),
        out_shape=jax.ShapeDtypeStruct((m, D_MODEL), F32),
        compiler_params=_cparams(1, 48),
        name="mlp",
    )(x, g, w_in, w_out)


QKV_TM = 256
QKV_COLS = N_GROUPS * 3 * D_MODEL


def _qkv_kernel(keep_rows, n_aliased, x_ref, g_ref, w_ref, *refs):
    qkv_ref, *kv_refs = refs[n_aliased:]
    h = _rms(x_ref[...], g_ref[...]).astype(BF16)
    tm = x_ref.shape[0]
    for g in range(N_GROUPS):
        for c in range(3):
            col = (g * 3 + c) * D_MODEL
            r = _dot(h, w_ref[:, col:col + D_MODEL])
            if c == 0:
                r = r * SCALE
            qkv_ref[:, col:col + D_MODEL] = r.astype(qkv_ref.dtype)
            if kv_refs and c > 0:
                kv_refs[g][:, (c - 1) * D_MODEL:c * D_MODEL] = r[tm - keep_rows[g]:, :]


def _qkv_proj(x, g, w, layer, *, batch=None, seq=None, kv_bufs=()):
    m = x.shape[0]
    prompt = batch is not None
    tm = QKV_TM if prompt else m
    in_specs = [pl.BlockSpec((tm, D_MODEL), lambda i: (i, 0)),
                _resident((1, D_MODEL)),
                _layer_slab((D_MODEL, QKV_COLS), layer)]
    out_specs = [pl.BlockSpec((tm, QKV_COLS), lambda i: (i, 0))]
    out_shape = [jax.ShapeDtypeStruct((m, QKV_COLS), BF16 if prompt else F32)]
    keep_rows = ()
    if prompt:
        tpb = seq // tm
        for win in WINDOWS:
            keep = min(win, seq)
            rows = min(keep, tm)
            first = tpb - keep // rows
            keep_rows += (rows,)
            out_specs.append(pl.BlockSpec(
                (None, None, rows, 2 * D_MODEL),
                lambda i, first=first: (layer, i // tpb, jnp.maximum(i % tpb - first, 0), 0)))
            out_shape.append(jax.ShapeDtypeStruct((DEPTH // 2, batch, keep, 2 * D_MODEL), F32))
    in_specs += [pl.BlockSpec(memory_space=pl.ANY)] * len(kv_bufs)
    return pl.pallas_call(
        functools.partial(_qkv_kernel, keep_rows, len(kv_bufs)),
        grid=(m // tm,),
        in_specs=in_specs, out_specs=out_specs, out_shape=out_shape,
        input_output_aliases={3 + k: 1 + k for k in range(len(kv_bufs))},
        compiler_params=_cparams(1, 56),
        name="qkv_proj",
    )(x, g, w, *kv_bufs)


MERGE_ROWS = 256


def _attn_prompt_kernel(*refs):
    qkv_refs, (tab_ref, o_ref, f_scr, og_scr, lg_scr) = refs[:9], refs[9:]
    seq = o_ref.shape[0]
    lo_q = lax.broadcasted_iota(jnp.int32, (BLK, LANES), 1) < HEAD_DIM
    m_lo, m_hi = {}, {}
    for n_rows in (BLK, 2 * BLK):
        lo = lax.broadcasted_iota(jnp.int32, (n_rows, LANES), 1) < HEAD_DIM
        m_lo[n_rows] = jnp.where(lo, 1.0, 0.0).astype(BF16)
        m_hi[n_rows] = jnp.where(lo, 0.0, 1.0).astype(BF16)

    for g in range(N_GROUPS):
        d = DILATIONS[g]
        if d == 1:
            src = qkv_refs[0:3]
        else:
            src = []
            for c in range(3):
                f_scr[(g - 1) * 3 + c] = qkv_refs[g * 3 + c][...].astype(F32)
                src.append(f_scr.at[(g - 1) * 3 + c])

        def rows(start, d=d):
            return pl.ds(start, BLK) if d == 1 else pl.ds(start, BLK, stride=d)

        def load(c, start, src=src):
            return src[c][rows(start), :].astype(BF16)

        n_blk = seq // d // BLK
        for r in range(d):
            for n in range(n_blk):
                start = r + n * BLK * d
                q = load(0, start)
                kc, vc = load(1, start), load(2, start)
                if n == 0:
                    keys, vals, tab = kc, vc, tab_ref[g, :, BLK:]
                else:
                    keys = jnp.concatenate([kp, kc], axis=0)
                    vals = jnp.concatenate([vp, vc], axis=0)
                    tab = tab_ref[g]
                kp, vp = kc, vc
                nk = keys.shape[0]
                q2 = jnp.concatenate([q * m_lo[BLK], q * m_hi[BLK]], axis=0)
                s = _dot_nt(q2, keys) + tab
                mx = jnp.max(s, axis=-1, keepdims=True)
                p = jnp.exp(s - mx).astype(BF16)
                pcat = jnp.concatenate([p[:BLK], p[BLK:]], axis=1)
                rhs = jnp.concatenate(
                    [jnp.concatenate([vals * m_lo[nk], m_lo[nk]], axis=1),
                     jnp.concatenate([vals * m_hi[nk], m_hi[nk]], axis=1)], axis=0)
                res = _dot(pcat, rhs)
                den = res[:, LANES:]
                og_scr[g, rows(start), :] = res[:, :LANES] / den
                lg_scr[g, rows(start), :] = jnp.where(lo_q, mx[:BLK], mx[BLK:]) + jnp.log(den)

    for r0 in range(0, seq, MERGE_ROWS):
        sl = slice(r0, r0 + MERGE_ROWS)
        l0, l1, l2 = lg_scr[0, sl, :], lg_scr[1, sl, :], lg_scr[2, sl, :]
        mx = jnp.maximum(jnp.maximum(l0, l1), l2)
        e0, e1, e2 = jnp.exp(l0 - mx), jnp.exp(l1 - mx), jnp.exp(l2 - mx)
        inv = 1.0 / (e0 + e1 + e2)
        o = (e0 * inv) * og_scr[0, sl, :] + (e1 * inv) * og_scr[1, sl, :] + (e2 * inv) * og_scr[2, sl, :]
        o_ref[sl, :] = o.astype(o_ref.dtype)


def _attn_prompt(qkv, tabs, batch, seq):
    m = batch * seq

    def spec(g, c):
        return pl.BlockSpec((seq, LANES), lambda b, p: (b, (g * 3 + c) * N_PAIRS + p))

    return pl.pallas_call(
        _attn_prompt_kernel,
        grid=(batch, N_PAIRS),
        in_specs=[spec(g, c) for g in range(N_GROUPS) for c in range(3)]
        + [pl.BlockSpec((N_GROUPS, None, 2 * BLK, 2 * BLK), lambda b, p: (0, p, 0, 0))],
        out_specs=pl.BlockSpec((seq, LANES), lambda b, p: (b, p)),
        out_shape=jax.ShapeDtypeStruct((m, D_MODEL), BF16),
        scratch_shapes=[pltpu.VMEM((6, seq, LANES), F32),
                        pltpu.VMEM((N_GROUPS, seq, LANES), F32),
                        pltpu.VMEM((N_GROUPS, seq, LANES), F32)],
        compiler_params=_cparams(2, 48),
        name="attn_prompt",
    )(*([qkv] * 9), tabs)


ATT_HB = SUBLANES
ATT_PAIRS = ATT_HB // 2


def _sample_attn_unit(q_refs, c_refs, t_refs, tn_refs):
    lane = lax.broadcasted_iota(jnp.int32, (HEAD_DIM, LANES), 1)
    head_of_lane = 2 * lax.broadcasted_iota(jnp.int32, (ATT_HB, LANES), 1)
    head_row = lax.broadcasted_iota(jnp.int32, (ATT_HB, LANES), 0)
    pad = jnp.zeros((LANES - 3 * ATT_PAIRS, LANES), F32)
    heads = range(ATT_HB)
    stack = lambda per_head: jnp.concatenate(per_head, axis=0)

    unnorm, dens, lses = [], [], []
    for q_ref, c_ref, t_ref, tn_ref in zip(q_refs, c_refs, t_refs, tn_refs):
        qt = jnp.concatenate([q_ref[...], pad], axis=0).T

        def col(c, hl, qt=qt):
            j = c * ATT_PAIRS + hl // 2
            return qt[(hl % 2) * HEAD_DIM:(hl % 2 + 1) * HEAD_DIM, j:j + 1]

        qb = [jnp.broadcast_to(col(0, hl), (HEAD_DIM, LANES)) for hl in heads]
        cks = [slice(k, k + LANES) for k in range(0, c_ref.shape[-1], LANES)]
        s_new = stack([jnp.sum(col(0, hl) * col(1, hl), axis=0, keepdims=True) for hl in heads]) + tn_ref[...]
        scores = [stack([jnp.sum(c_ref[0, hl, :, ck] * qb[hl], axis=0, keepdims=True) for hl in heads])
                  + t_ref[:, ck] for ck in cks]
        mx = jnp.max(functools.reduce(jnp.maximum, scores), axis=-1, keepdims=True)
        mx = jnp.maximum(mx, s_new)
        p = [jnp.exp(sc - mx) for sc in scores]
        pn = jnp.exp(s_new - mx)
        den = jnp.sum(functools.reduce(jnp.add, p), axis=-1, keepdims=True) + pn
        halves = [jnp.zeros((HEAD_DIM, LANES), F32) for _ in range(2)]
        for hl in heads:
            acc = functools.reduce(jnp.add, [c_ref[1, hl, :, ck] * pc[hl:hl + 1, :]
                                             for ck, pc in zip(cks, p)])
            un = jnp.sum(acc, axis=-1, keepdims=True) + pn[hl:hl + 1, 0:1] * col(2, hl)
            halves[hl % 2] = jnp.where(lane == hl // 2, un, halves[hl % 2])
        unnorm.append(halves)
        dens.append(den)
        lses.append(mx + jnp.log(den))

    lmx = jnp.maximum(jnp.maximum(lses[0], lses[1]), lses[2])
    es = [jnp.exp(l - lmx) for l in lses]
    tot = es[0] + es[1] + es[2]
    coef = [e / (tot * dn) for e, dn in zip(es, dens)]
    out = []
    for parity in range(2):
        o = None
        for g in range(N_GROUPS):
            c_row = jnp.sum(jnp.where(head_row == head_of_lane + parity, coef[g], 0.0), axis=0, keepdims=True)
            o = unnorm[g][parity] * c_row if o is None else o + unnorm[g][parity] * c_row
        out.append(o)
    return jnp.concatenate(out, axis=0).T[0:ATT_PAIRS, :]


ATT_NHB = N_HEADS // ATT_HB
MLP_CHUNKS = D_FF // MLP_FC
MLP_UNITS = MLP_CHUNKS
TILE_ROWS = MLP_UNITS // ATT_NHB
SLAB_DMA_PRIORITY = 1
assert MLP_UNITS % 2 == 0 and MLP_UNITS % ATT_NHB == 0


def _slab_copies(layer, row, k, c_hbms, bufs, sems):
    heads = pl.ds((k % ATT_NHB) * ATT_HB, ATT_HB)
    return [pltpu.make_async_copy(c_hbm.at[layer, row + k // ATT_NHB, :, heads], buf.at[k % 2], sems.at[g, k % 2])
            for g, (c_hbm, buf) in enumerate(zip(c_hbms, bufs))]


def _attn_operands(qkv_s, caches_t, tabs, tab_new, u_first, n_units):
    db = qkv_s.shape[0]
    assert u_first % ATT_NHB == 0 and n_units % ATT_NHB == 0
    b0, rows = u_first // ATT_NHB, n_units // ATT_NHB
    q = qkv_s[b0:b0 + rows].reshape(rows, N_GROUPS, 3, ATT_NHB, ATT_PAIRS, LANES).transpose(1, 0, 3, 2, 4, 5)
    q = q.reshape(N_GROUPS, rows, ATT_NHB, 3 * ATT_PAIRS, LANES)
    args = [q, *[t.reshape(ATT_NHB, ATT_HB, -1) for t in tabs],
            tab_new.reshape(N_GROUPS, ATT_NHB, ATT_HB, LANES), *caches_t]
    specs = [pl.BlockSpec((N_GROUPS, TILE_ROWS) + q.shape[2:], lambda i: (0, i, 0, 0, 0)),
             *[_resident((ATT_NHB, ATT_HB, w)) for w in WINDOWS],
             _resident((N_GROUPS, ATT_NHB, ATT_HB, LANES)), *[pl.BlockSpec(memory_space=pl.ANY)] * N_GROUPS]
    out_shape = jax.ShapeDtypeStruct((rows, ATT_NHB, ATT_PAIRS, LANES), F32)
    scratch = [*[pltpu.VMEM((2, 2, ATT_HB, HEAD_DIM, w), F32) for w in WINDOWS],
               pltpu.SemaphoreType.DMA((N_GROUPS, 2))]
    return args, specs, out_shape, scratch


def _mlp_attn_kernel(layer, u_first, final_norm, has_proj, x_ref, g_ref, gf_ref, win_ref, wout_ref, *refs):
    if has_proj:
        oin_ref, wo_ref, *refs = refs
    (q_ref, t0_ref, t1_ref, t2_ref, tn_ref, c0_hbm, c1_hbm, c2_hbm, o_ref, ao_ref,
     buf0, buf1, buf2, sems) = refs
    i = pl.program_id(0)
    n_steps = pl.num_programs(0)
    c_hbms, bufs, t_refs = (c0_hbm, c1_hbm, c2_hbm), (buf0, buf1, buf2), (t0_ref, t1_ref, t2_ref)
    copies = functools.partial(_slab_copies, layer, c_hbms=c_hbms, bufs=bufs, sems=sems)
    row = u_first // ATT_NHB + i * TILE_ROWS

    @pl.when(i == 0)
    def _():
        for cp in copies(row, 0):
            cp.start(priority=SLAB_DMA_PRIORITY)

    x = x_ref[...]
    if has_proj:
        x = x + _dot(oin_ref[...].astype(BF16), wo_ref[...])
    h = _rms(x, g_ref[...]).astype(BF16)
    acc = x
    for k in range(MLP_CHUNKS):
        if k + 1 < MLP_UNITS:
            for cp in copies(row, k + 1):
                cp.start(priority=SLAB_DMA_PRIORITY)
        else:
            @pl.when(i + 1 < n_steps)
            def _():
                for cp in copies(row + TILE_ROWS, 0):
                    cp.start(priority=SLAB_DMA_PRIORITY)
        for cp in copies(row, k):
            cp.wait()
        c0 = k * MLP_FC
        u = jnp.maximum(_dot(h, win_ref[:, c0:c0 + MLP_FC]), 0.0)
        acc = acc + _dot((u * u).astype(BF16), wout_ref[c0:c0 + MLP_FC, :])
        bl, hb = k // ATT_NHB, k % ATT_NHB
        ao_ref[bl, hb] = _sample_attn_unit(
            [q_ref.at[g, bl, hb] for g in range(N_GROUPS)],
            [buf.at[k % 2] for buf in bufs],
            [t_ref.at[hb] for t_ref in t_refs],
            [tn_ref.at[g, hb] for g in range(N_GROUPS)])
    o_ref[...] = _rms(acc, gf_ref[...]) if final_norm else acc


def _mlp_attn(x, g, g_final, w_in, w_out, w_layer, qkv_s, caches_t, cache_layer, tabs, tab_new, u_first,
              final_norm, proj=None):
    m = x.shape[0]
    tm = _row_tile(m)
    n_units = (m // tm) * MLP_UNITS
    a_args, a_specs, a_shape, a_scratch = _attn_operands(qkv_s, caches_t, tabs, tab_new, u_first, n_units)
    row = pl.BlockSpec((tm, D_MODEL), lambda i: (i, 0))
    p_args, p_specs = ([], []) if proj is None else (list(proj[:2]), [row, _layer_slab((D_MODEL, D_MODEL), proj[2])])
    xo, ao = pl.pallas_call(
        functools.partial(_mlp_attn_kernel, cache_layer, u_first, final_norm, proj is not None),
        grid=(m // tm,),
        in_specs=[row, _resident((1, D_MODEL)), _resident((1, D_MODEL)),
                  _layer_slab((D_MODEL, D_FF), w_layer), _layer_slab((D_FF, D_MODEL), w_layer),
                  *p_specs, *a_specs],
        out_specs=[row, pl.BlockSpec((TILE_ROWS,) + a_shape.shape[1:], lambda i: (i, 0, 0, 0))],
        out_shape=[jax.ShapeDtypeStruct((m, D_MODEL), F32), a_shape],
        scratch_shapes=a_scratch,
        compiler_params=_cparams(1, 58),
        name="mlp_attn",
    )(x, g, g_final, w_in, w_out, *p_args, *a_args)
    return xo, ao.reshape(-1, D_MODEL)


def _proj_kernel(o_ref, x_ref, w_ref, out_ref):
    out_ref[...] = x_ref[...] + _dot(o_ref[...].astype(BF16), w_ref[...])


def _proj_residual(o, x, w, layer):
    m = x.shape[0]
    tm = _row_tile(m, 1024)
    row = pl.BlockSpec((tm, D_MODEL), lambda i: (i, 0))
    return pl.pallas_call(
        _proj_kernel,
        grid=(m // tm,),
        in_specs=[row, row, _layer_slab((D_MODEL, D_MODEL), layer)],
        out_specs=row,
        out_shape=jax.ShapeDtypeStruct((m, D_MODEL), F32),
        compiler_params=_cparams(1, 40),
        name="proj_residual",
    )(o, x, w)


def _norm_kernel(x_ref, g_ref, o_ref):
    o_ref[...] = _rms(x_ref[...], g_ref[...])


def _final_norm(x, g):
    m = x.shape[0]
    tm = _row_tile(m, 1024)
    return pl.pallas_call(
        _norm_kernel,
        grid=(m // tm,),
        in_specs=[pl.BlockSpec((tm, D_MODEL), lambda i: (i, 0)), _resident((1, D_MODEL))],
        out_specs=pl.BlockSpec((tm, D_MODEL), lambda i: (i, 0)),
        out_shape=jax.ShapeDtypeStruct((m, D_MODEL), F32),
        compiler_params=_cparams(1, 32),
        name="final_norm",
    )(x, g)


def _t5_bucket(dist):
    max_exact = NUM_BUCKETS // 2
    n = jnp.maximum(dist.astype(F32), 1.0)
    large = max_exact + (jnp.log(n / max_exact) / math.log(MAX_DISTANCE / max_exact)
                         * (NUM_BUCKETS - max_exact)).astype(jnp.int32)
    large = jnp.minimum(large, NUM_BUCKETS - 1)
    return jnp.where(dist < max_exact, dist, large)


def _toeplitz(w):
    n = 2 * BLK
    tiled = jnp.tile(w, (1,) * (w.ndim - 1) + (BLK,))[..., :BLK * (n - 1)]
    return tiled.reshape(w.shape[:-1] + (BLK, n - 1))[..., :BLK]


def _bias_tables(rel_bias):
    neg_run = jnp.full((N_HEADS, BLK - 1), NEG, F32)
    neg_one = jnp.full((N_HEADS, 1), NEG, F32)
    prompt, tab, tab_new = [], [], []
    for g in range(N_GROUPS):
        d = DILATIONS[g]
        dist = jnp.arange(BLK + 1, dtype=jnp.int32) * d
        tbl = rel_bias[:, g * N_HEADS:(g + 1) * N_HEADS].astype(F32)
        bias = tbl[_t5_bucket(dist)].T
        rev = bias[:, ::-1]
        w_cur = jnp.concatenate([bias[:, 0:1], neg_run, neg_one, rev[:, 1:BLK]], axis=1)
        w_prev = jnp.concatenate([rev[:, 0:BLK], neg_one, neg_run], axis=1)
        t = jnp.concatenate([_toeplitz(w_prev), _toeplitz(w_cur)], axis=-1)
        prompt.append(t.reshape(N_PAIRS, 2 * BLK, 2 * BLK))
        on_grid = rev[:, 0:BLK, None]
        off_grid = jnp.full((N_HEADS, BLK, d - 1), NEG, F32)
        tab.append(jnp.concatenate([on_grid, off_grid], axis=2).reshape(N_HEADS, BLK * d))
        tab_new.append(jnp.broadcast_to(bias[:, 0:1], (N_HEADS, LANES)))
    return jnp.stack(prompt), tab, jnp.stack(tab_new)


def kernel(x_prompt, x_sample, state_conv, cache_kv_w128, cache_kv_w512, cache_kv_w2048,
           norm_mix, norm_mlp, norm_final,
           conv_w_pw1, conv_b_pw1, conv_w_dw, conv_b_dw, conv_ln_g, conv_ln_b, conv_w_pw2, conv_b_pw2,
           attn_w_qkv, attn_w_o, rel_bias, mlp_w_in, mlp_w_out):
    batch, seq, _ = x_prompt.shape
    db = x_sample.shape[0]
    assert x_sample.shape[1] == 1 and seq % (DILATIONS[-1] * BLK) == 0
    caches = (cache_kv_w128, cache_kv_w512, cache_kv_w2048)
    assert all(c.shape[2] == w for c, w in zip(caches, WINDOWS))

    xp = x_prompt.reshape(batch * seq, D_MODEL)
    xs = x_sample.reshape(db, D_MODEL)
    row = lambda v: v.reshape(1, -1)

    tabs_p, tabs_s, tab_new = _bias_tables(rel_bias)
    caches_t = [jnp.transpose(c, (0, 1, 3, 4, 5, 2)) for c in caches]
    state_t = jnp.transpose(state_conv, (0, 2, 1, 3))

    w_pw1, w_pw2 = conv_w_pw1.astype(BF16), conv_w_pw2.astype(BF16)
    w_qkv, w_o = attn_w_qkv.astype(BF16), attn_w_o.astype(BF16)
    w_in, w_out = mlp_w_in.astype(BF16), mlp_w_out.astype(BF16)

    conv_p, conv_s = [], []
    kv_bufs = ()
    kv_s = [[] for _ in range(N_GROUPS)]
    m_rows = batch * seq
    u_mlp = (m_rows // _row_tile(m_rows)) * MLP_UNITS
    assert 2 * u_mlp == db * ATT_NHB
    g_final = row(norm_final)
    for j in range(DEPTH // 2):
        ic, ia = 2 * j, 2 * j + 1
        last = ia == DEPTH - 1
        g_mix = row(norm_mix[ic])
        prm = (j, conv_w_dw[j], row(conv_b_dw[j]), row(conv_ln_g[j]), row(conv_ln_b[j]),
               w_pw2, row(conv_b_pw2[j]))
        a_p = _pw1_glu(xp, g_mix, w_pw1, row(conv_b_pw1[j]), j)
        a_s = _pw1_glu(xs, g_mix, w_pw1, row(conv_b_pw1[j]), j)
        xp = _conv_prompt(a_p, xp, batch, seq, *prm)
        xs, ns = _conv_sample(a_s, state_t, xs, *prm)
        conv_p.append(a_p.reshape(batch, seq, D_MODEL)[:, seq - HIST:])
        conv_s.append(jnp.transpose(ns, (1, 0, 2)))
        xs = _mlp(xs, row(norm_mlp[ic]), w_in, w_out, ic)
        g_mix = row(norm_mix[ia])
        (qkv_s,) = _qkv_proj(xs, g_mix, w_qkv, j)
        for g in range(N_GROUPS):
            kv_new = qkv_s[:, (g * 3 + 1) * D_MODEL:(g * 3 + 3) * D_MODEL]
            kv_s[g].append(kv_new.reshape(db, 1, 2, N_HEADS, HEAD_DIM))
        att = (qkv_s, caches_t, j, tabs_s, tab_new)
        xp, o_a = _mlp_attn(xp, row(norm_mlp[ic]), g_final, w_in, w_out, ic, *att, 0, False)
        qkv_p, *kv_bufs = _qkv_proj(xp, g_mix, w_qkv, j, batch=batch, seq=seq, kv_bufs=kv_bufs)
        o_p = _attn_prompt(qkv_p, tabs_p, batch, seq)
        xp, o_b = _mlp_attn(xp, row(norm_mlp[ia]), g_final, w_in, w_out, ia, *att, u_mlp, last,
                            proj=(o_p, w_o, j))
        xs = _proj_residual(jnp.concatenate([o_a, o_b], axis=0), xs, w_o, j)
        xs = _mlp(xs, row(norm_mlp[ia]), w_in, w_out, ia)

    y_prompt = xp.reshape(batch, seq, D_MODEL)
    y_sample = _final_norm(xs, g_final).reshape(db, 1, D_MODEL)
    kv_p = [b.reshape(b.shape[:3] + (2, N_HEADS, HEAD_DIM)) for b in kv_bufs]
    return (y_prompt, y_sample, jnp.stack(conv_p), jnp.stack(conv_s),
            kv_p[0], jnp.stack(kv_s[0]), kv_p[1], jnp.stack(kv_s[1]), kv_p[2], jnp.stack(kv_s[2]))
```
